```python
import jax
import jax.numpy as jnp
from jax import lax
import numpy as np

D_MODEL = 2048
BATCH = 4
SEQ = 2048
DEPTH = 2
DEC_BATCH = 8
DEC_SEQ = 4
PAST_LEN = 16384
PAGE_SIZE = 128

N_A_LAYERS = DEPTH // 2
N_B_LAYERS = DEPTH - N_A_LAYERS
CONV_W = 3
N_HEADS = 16
KV_HEADS = 2
GROUP = N_HEADS // KV_HEADS
HEAD_DIM = D_MODEL // N_HEADS
ROT_DIM = HEAD_DIM // 4
ROPE_THETA = 500000.0
N_BRANCH = 3
N_KV_SLOTS = 2 * N_BRANCH
N_PAGED_SLOTS = 4
N_WIN_SLOTS = N_KV_SLOTS - N_PAGED_SLOTS
CMP_LEN = 32
CMP_STRIDE = 16
CMP_RATIO = CMP_LEN // CMP_STRIDE
CMP_HIDDEN = HEAD_DIM
SEL_LEN = 64
N_SEL = 16
WINDOW = 512
Q_BLOCK = 64
N_EXPERTS = 32
TOP_K = 4
D_FF = D_MODEL
SWIGLU_LIMIT = 7.0
SWIGLU_ALPHA = 1.702
MOE_BLOCK = 128
EPS = 1e-6

kernel_name = 'yoco_shortconv_nsa_moe_adaln_step'


def rmsnorm(x, g):
    xf = x.astype(jnp.float32)
    y = xf * lax.rsqrt(jnp.mean(xf * xf, axis=-1, keepdims=True) + EPS)
    return (y * g.astype(jnp.float32)).astype(x.dtype)


def modulate(x, g, shift, scale):
    return rmsnorm(x, g) * (1 + scale[:, None, :]) + shift[:, None, :]


def ada_mod(c, w, b):
    return jnp.einsum('bd,lsde->blse', jax.nn.silu(c), w) + b


def rotary(x, pos):
    inv = ROPE_THETA ** (-jnp.arange(0, ROT_DIM, 2, dtype=jnp.float32) / ROT_DIM)
    ang = pos.astype(jnp.float32)[:, None] * inv[None, :]
    bshape = (1, pos.shape[0]) + (1,) * (x.ndim - 3) + (ROT_DIM // 2,)
    cos = jnp.cos(ang).reshape(bshape)
    sin = jnp.sin(ang).reshape(bshape)
    xr = x[..., :ROT_DIM].astype(jnp.float32)
    x1, x2 = xr[..., :ROT_DIM // 2], xr[..., ROT_DIM // 2:]
    rot = jnp.concatenate([x1 * cos - x2 * sin, x2 * cos + x1 * sin], axis=-1)
    return jnp.concatenate([rot.astype(x.dtype), x[..., ROT_DIM:]], axis=-1)


def masked_softmax(s, mask):
    s = jnp.where(mask, s.astype(jnp.float32), -jnp.inf)
    m = jnp.max(s, axis=-1, keepdims=True)
    m = jnp.where(jnp.isfinite(m), m, 0.0)
    p = jnp.exp(s - m)
    den = jnp.sum(p, axis=-1, keepdims=True)
    return p / jnp.where(den > 0, den, 1.0)


def short_conv_mixer(h, w_in, w_conv, w_out, conv_state):
    t = h.shape[1]
    b, c, xt = jnp.split(h @ w_in, 3, axis=-1)
    u = b * xt
    u_pad = jnp.concatenate([conv_state.astype(u.dtype), u], axis=1)
    conv = sum(w_conv[k] * u_pad[:, k:k + t] for k in range(CONV_W))
    return (c * conv) @ w_out, u_pad[:, -(CONV_W - 1):]


def shared_kv_rows(s, w_kv, pos):
    bq, t = s.shape[:2]
    kv = (s @ w_kv).reshape(bq, t, N_BRANCH, 2, KV_HEADS, HEAD_DIM)
    k = rotary(kv[:, :, :, 0], pos)
    return jnp.stack([k, kv[:, :, :, 1]], axis=3).reshape(bq, t, N_KV_SLOTS, KV_HEADS, HEAD_DIM)


def compress(x_full, w1, w2, pe):
    bq, length = x_full.shape[:2]
    n_chunk = length // CMP_STRIDE
    n_cmp = n_chunk - CMP_RATIO + 1
    chunks = x_full[:, :n_chunk * CMP_STRIDE].reshape(bq, n_chunk, CMP_STRIDE, KV_HEADS, HEAD_DIM)
    w1r = w1.reshape(CMP_RATIO, CMP_STRIDE, HEAD_DIM, CMP_HIDDEN)
    per = pe.reshape(CMP_RATIO, CMP_STRIDE, HEAD_DIM)
    hid = sum(jnp.einsum('bnsgd,sde->bnge', chunks[:, r:r + n_cmp] + per[r][:, None, :], w1r[r])
              for r in range(CMP_RATIO))
    return jnp.einsum('bnge,ed->bngd', jax.nn.silu(hid), w2)


def prepare_context(full, w_cmp1, w_cmp2, pe_cmp):
    bq, length = full.shape[:2]
    kc = compress(full[:, :, 0], w_cmp1[0], w_cmp2[0], pe_cmp[0])
    vc = compress(full[:, :, 1], w_cmp1[1], w_cmp2[1], pe_cmp[1])
    n_cmp = kc.shape[1]
    cmp_end = jnp.arange(n_cmp) * CMP_STRIDE + CMP_LEN - 1
    nb = -(-length // SEL_LEN)
    sel = jnp.pad(full[:, :, 2:4], ((0, 0), (0, nb * SEL_LEN - length), (0, 0), (0, 0), (0, 0)))
    sel = sel.reshape(bq, nb, SEL_LEN, 2, KV_HEADS, HEAD_DIM).transpose(3, 0, 4, 1, 2, 5)
    ci = jnp.arange(n_cmp)[:, None] * CMP_STRIDE
    sj = jnp.arange(nb)[None, :] * SEL_LEN
    ov = ((ci <= sj + SEL_LEN - 1) & (ci + CMP_LEN - 1 >= sj)).astype(jnp.float32)
    return (kc, vc, cmp_end, ov, sel[0], sel[1])


def nsa_attend(q, gates, tq, kc, vc, cmp_end, ov, ksb, vsb, kw, vw, pos_w):
    bq, nq = q.shape[:2]
    qg = q.reshape(bq, nq, KV_HEADS, GROUP, HEAD_DIM)
    scale = HEAD_DIM ** -0.5
    s_c = jnp.einsum('bqghd,bngd->bqghn', qg, kc) * scale
    p_c = masked_softmax(s_c, (cmp_end[None, :] <= tq[:, None])[None, :, None, None, :])
    o_c = jnp.einsum('bqghn,bngd->bqghd', p_c.astype(vc.dtype), vc)
    imp = jnp.einsum('bqgn,nj->bqgj', p_c.sum(axis=3), ov)
    nb = ov.shape[1]
    j = jnp.arange(nb)[None, :]
    bt = (tq // SEL_LEN)[:, None]
    forced = (j == 0) | (j == bt) | (j == bt - 1)
    imp = jnp.where(forced[None, :, None, :], jnp.inf,
                    jnp.where((j <= bt)[None, :, None, :], imp, -jnp.inf))
    _, idx = lax.top_k(imp, min(N_SEL, nb))
    kk = idx.shape[-1]
    idx_t = idx.transpose(0, 2, 1, 3)
    gather = jax.vmap(jax.vmap(lambda blk, ix: blk[ix]))
    kg = gather(ksb, idx_t)
    vg = gather(vsb, idx_t)
    s_s = jnp.einsum('bqghd,bgqnsd->bgqhns', qg, kg) * scale
    kpos = idx_t[..., None] * SEL_LEN + jnp.arange(SEL_LEN)
    m_s = (kpos <= tq[None, None, :, None, None]).reshape(bq, KV_HEADS, nq, 1, kk * SEL_LEN)
    p_s = masked_softmax(s_s.reshape(bq, KV_HEADS, nq, GROUP, kk * SEL_LEN), m_s)
    o_s = jnp.einsum('bgqhm,bgqmd->bqghd', p_s.astype(vg.dtype),
                     vg.reshape(bq, KV_HEADS, nq, kk * SEL_LEN, HEAD_DIM))
    s_w = jnp.einsum('bqghd,bkgd->bqghk', qg, kw) * scale
    dt = tq[:, None] - pos_w[None, :]
    m_w = (dt >= 0) & (dt < WINDOW) & (pos_w[None, :] >= 0)
    p_w = masked_softmax(s_w, m_w[None, :, None, None, :])
    o_w = jnp.einsum('bqghk,bkgd->bqghd', p_w.astype(vw.dtype), vw)
    o = jnp.stack([o_c, o_s, o_w], axis=-1).reshape(bq, nq, N_HEADS, HEAD_DIM, N_BRANCH)
    return jnp.einsum('bqhdr,bqhr->bqhd', o.astype(jnp.float32), gates).astype(q.dtype)


def query_side(h, pos, w_qg):
    bq, t = h.shape[:2]
    qg = h @ w_qg
    q = rotary(qg[..., :N_HEADS * HEAD_DIM].reshape(bq, t, N_HEADS, HEAD_DIM), pos)
    gates = jax.nn.sigmoid(qg[..., N_HEADS * HEAD_DIM:].astype(jnp.float32)).reshape(bq, t, N_HEADS, N_BRANCH)
    return q, gates


def nsa_prompt(h, pos, w_qg, w_o, ctx, win_k, win_v):
    bq, t = h.shape[:2]
    q, gates = query_side(h, pos, w_qg)
    pad = ((0, 0), (WINDOW, 0), (0, 0), (0, 0))
    kw_pad = jnp.pad(win_k, pad)
    vw_pad = jnp.pad(win_v, pad)
    pos_pad = jnp.arange(-WINDOW, t)
    span = WINDOW + Q_BLOCK

    def block(i):
        s0 = i * Q_BLOCK
        return nsa_attend(lax.dynamic_slice_in_dim(q, s0, Q_BLOCK, 1),
                          lax.dynamic_slice_in_dim(gates, s0, Q_BLOCK, 1),
                          s0 + jnp.arange(Q_BLOCK), *ctx,
                          lax.dynamic_slice_in_dim(kw_pad, s0, span, 1),
                          lax.dynamic_slice_in_dim(vw_pad, s0, span, 1),
                          lax.dynamic_slice_in_dim(pos_pad, s0, span, 0))

    o = lax.map(block, jnp.arange(t // Q_BLOCK))
    o = o.transpose(1, 0, 2, 3, 4).reshape(bq, t, N_HEADS * HEAD_DIM)
    return o @ w_o


def nsa_sample(h, pos, w_qg, w_o, ctx, win_k, win_v, pos_w):
    bq, t = h.shape[:2]
    q, gates = query_side(h, pos, w_qg)
    o = nsa_attend(q, gates, pos, *ctx, win_k, win_v, pos_w)
    return o.reshape(bq, t, N_HEADS * HEAD_DIM) @ w_o


def moe(h, w_router, b_router, w_gu, b_gu, w_down, b_down):
    n_tok = h.shape[0]
    logits = jnp.dot(h.astype(jnp.float32), w_router.astype(jnp.float32)) + b_router.astype(jnp.float32)
    top_v, top_i = lax.top_k(logits, TOP_K)
    wts = jax.nn.softmax(top_v, axis=-1)
    nk = n_tok * TOP_K
    e_flat = top_i.reshape(nk)
    tok_flat = jnp.repeat(jnp.arange(n_tok, dtype=jnp.int32), TOP_K)
    order = jnp.argsort(e_flat)
    se, st, sw = e_flat[order], tok_flat[order], wts.reshape(nk)[order]
    counts = jnp.zeros((N_EXPERTS,), jnp.int32).at[e_flat].add(1)
    starts = jnp.cumsum(counts) - counts
    nblk_e = (counts + MOE_BLOCK - 1) // MOE_BLOCK
    blk_end = jnp.cumsum(nblk_e)
    pstarts = (blk_end - nblk_e) * MOE_BLOCK
    dest = pstarts[se] + jnp.arange(nk, dtype=jnp.int32) - starts[se]
    n_blk = -(-nk // MOE_BLOCK) + N_EXPERTS
    slot_tok = jnp.zeros((n_blk * MOE_BLOCK,), jnp.int32).at[dest].set(st)
    slot_w = jnp.zeros((n_blk * MOE_BLOCK,), jnp.float32).at[dest].set(sw)
    blk_expert = jnp.minimum(jnp.searchsorted(blk_end, jnp.arange(n_blk, dtype=jnp.int32), side='right'),
                             N_EXPERTS - 1)
    xs = h[slot_tok].reshape(n_blk, MOE_BLOCK, h.shape[1])

    def expert_block(args):
        xb, e = args
        gate, up = jnp.split(xb @ w_gu[e] + b_gu[e], 2, axis=-1)
        gate = jnp.minimum(gate, SWIGLU_LIMIT)
        up = jnp.clip(up, -SWIGLU_LIMIT, SWIGLU_LIMIT)
        act = (up + 1) * (gate * jax.nn.sigmoid(SWIGLU_ALPHA * gate))
        return act @ w_down[e] + b_down[e]

    ys = lax.map(expert_block, (xs, blk_expert)).reshape(n_blk * MOE_BLOCK, h.shape[1])
    out = jax.ops.segment_sum(ys.astype(jnp.float32) * slot_w[:, None], slot_tok, num_segments=n_tok)
    return out.astype(h.dtype)


def setup_inputs(seed: int = 0) -> dict:
    key = jax.random.key(seed)
    ks = iter(jax.random.split(key, 40))

    def nrm(shape, s):
        return jax.random.normal(next(ks), shape, jnp.float32) * s

    d = D_MODEL
    hd = N_HEADS * HEAD_DIM
    n_pages = PAST_LEN // PAGE_SIZE
    n_used = DEC_BATCH * n_pages
    n_phys = n_used + (n_used + 3) // 4
    page_table = jax.random.permutation(next(ks), n_phys)[:n_used].reshape(DEC_BATCH, n_pages).astype(jnp.int32)
    return {
        'x_prompt': nrm((BATCH, SEQ, d), 1.0),
        'x_sample': nrm((DEC_BATCH, DEC_SEQ, d), 1.0),
        'cache_kv': nrm((n_phys, PAGE_SIZE, N_PAGED_SLOTS, KV_HEADS, HEAD_DIM), 1.0),
        'cache_win': nrm((DEC_BATCH, min(WINDOW, PAST_LEN), N_WIN_SLOTS, KV_HEADS, HEAD_DIM), 1.0),
        'state_conv': nrm((N_A_LAYERS, DEC_BATCH, CONV_W - 1, d), 1.0),
        'page_table': page_table,
        'c_prompt': nrm((BATCH, d), 1.0),
        'c_sample': nrm((DEC_BATCH, d), 1.0),
        'norm_g': 1.0 + nrm((DEPTH, 2, d), 0.02),
        'w_ada': nrm((DEPTH, 2, d, 3 * d), 0.5 * d ** -0.5),
        'b_ada': nrm((DEPTH, 2, 3 * d), 0.02),
        'w_a_in': nrm((N_A_LAYERS, d, 3 * d), d ** -0.5),
        'w_a_conv': nrm((N_A_LAYERS, CONV_W, d), CONV_W ** -0.5),
        'w_a_out': nrm((N_A_LAYERS, d, d), d ** -0.5),
        'norm_kv': 1.0 + nrm((d,), 0.02),
        'w_ada_kv': nrm((d, 2 * d), 0.5 * d ** -0.5),
        'b_ada_kv': nrm((2 * d,), 0.02),
        'w_kv': nrm((d, N_KV_SLOTS * KV_HEADS * HEAD_DIM), d ** -0.5),
        'w_cmp1': nrm((2, CMP_LEN, HEAD_DIM, CMP_HIDDEN), (CMP_LEN * HEAD_DIM) ** -0.5),
        'w_cmp2': nrm((2, CMP_HIDDEN, HEAD_DIM), CMP_HIDDEN ** -0.5),
        'pe_cmp': nrm((2, CMP_LEN, HEAD_DIM), 0.1),
        'w_b_qg': nrm((N_B_LAYERS, d, hd + N_HEADS * N_BRANCH), d ** -0.5),
        'w_b_o': nrm((N_B_LAYERS, hd, d), hd ** -0.5),
        'w_router': nrm((DEPTH, d, N_EXPERTS), d ** -0.5),
        'b_router': nrm((DEPTH, N_EXPERTS), 0.01),
        'w_gu': nrm((DEPTH, N_EXPERTS, d, 2 * D_FF), d ** -0.5),
        'b_gu': nrm((DEPTH, N_EXPERTS, 2 * D_FF), 0.01),
        'w_down': nrm((DEPTH, N_EXPERTS, D_FF, d), D_FF ** -0.5),
        'b_down': nrm((DEPTH, N_EXPERTS, d), 0.01),
        'norm_f': 1.0 + nrm((d,), 0.02),
    }


def reference(x_prompt, x_sample, cache_kv, cache_win, state_conv, page_table, c_prompt, c_sample,
              norm_g, w_ada, b_ada, w_a_in, w_a_conv, w_a_out, norm_kv, w_ada_kv, b_ada_kv, w_kv,
              w_cmp1, w_cmp2, pe_cmp, w_b_qg, w_b_o, w_router, b_router, w_gu, b_gu, w_down, b_down,
              norm_f):
    bp, tp, d = x_prompt.shape
    bs, ts, _ = x_sample.shape
    past_len = page_table.shape[1] * PAGE_SIZE
    pos_p = jnp.arange(tp)
    pos_s = past_len + jnp.arange(ts)
    mod_p = ada_mod(c_prompt, w_ada, b_ada)
    mod_s = ada_mod(c_sample, w_ada, b_ada)
    xp, xs = x_prompt, x_sample
    conv_p, conv_s = [], []
    for layer in range(DEPTH):
        sh_p, sc_p, g_p = jnp.split(mod_p[:, layer, 0], 3, axis=-1)
        sh_s, sc_s, g_s = jnp.split(mod_s[:, layer, 0], 3, axis=-1)
        hp = modulate(xp, norm_g[layer, 0], sh_p, sc_p)
        hs = modulate(xs, norm_g[layer, 0], sh_s, sc_s)
        if layer < N_A_LAYERS:
            ya_p, st_p = short_conv_mixer(hp, w_a_in[layer], w_a_conv[layer], w_a_out[layer],
                                          jnp.zeros((bp, CONV_W - 1, d), hp.dtype))
            ya_s, st_s = short_conv_mixer(hs, w_a_in[layer], w_a_conv[layer], w_a_out[layer], state_conv[layer])
            conv_p.append(st_p)
            conv_s.append(st_s)
        else:
            if layer == N_A_LAYERS:
                shkv_p, sckv_p = jnp.split(jax.nn.silu(c_prompt) @ w_ada_kv + b_ada_kv, 2, axis=-1)
                shkv_s, sckv_s = jnp.split(jax.nn.silu(c_sample) @ w_ada_kv + b_ada_kv, 2, axis=-1)
                kv_p = shared_kv_rows(modulate(xp, norm_kv, shkv_p, sckv_p), w_kv, pos_p)
                kv_s = shared_kv_rows(modulate(xs, norm_kv, shkv_s, sckv_s), w_kv, pos_s)
                ctx_p = prepare_context(kv_p[:, :, :N_PAGED_SLOTS], w_cmp1, w_cmp2, pe_cmp)
                past = cache_kv[page_table].reshape(bs, past_len, N_PAGED_SLOTS, KV_HEADS, HEAD_DIM)
                ctx_s = prepare_context(jnp.concatenate([past, kv_s[:, :, :N_PAGED_SLOTS]], axis=1),
                                        w_cmp1, w_cmp2, pe_cmp)
                win_full_s = jnp.concatenate([cache_win, kv_s[:, :, N_PAGED_SLOTS:]], axis=1)
                n_buf = cache_win.shape[1]
                pos_w_s = past_len - n_buf + jnp.arange(n_buf + ts)
            lb = layer - N_A_LAYERS
            ya_p = nsa_prompt(hp, pos_p, w_b_qg[lb], w_b_o[lb], ctx_p,
                              kv_p[:, :, N_PAGED_SLOTS], kv_p[:, :, N_PAGED_SLOTS + 1])
            ya_s = nsa_sample(hs, pos_s, w_b_qg[lb], w_b_o[lb], ctx_s,
                              win_full_s[:, :, 0], win_full_s[:, :, 1], pos_w_s)
        xp = xp + g_p[:, None, :] * ya_p
        xs = xs + g_s[:, None, :] * ya_s
        sh_p, sc_p, g_p = jnp.split(mod_p[:, layer, 1], 3, axis=-1)
        sh_s, sc_s, g_s = jnp.split(mod_s[:, layer, 1], 3, axis=-1)
        hp = modulate(xp, norm_g[layer, 1], sh_p, sc_p)
        hs = modulate(xs, norm_g[layer, 1], sh_s, sc_s)
        y_ff = moe(jnp.concatenate([hp.reshape(bp * tp, d), hs.reshape(bs * ts, d)], axis=0),
                   w_router[layer], b_router[layer], w_gu[layer], b_gu[layer], w_down[layer], b_down[layer])
        xp = xp + g_p[:, None, :] * y_ff[:bp * tp].reshape(bp, tp, d)
        xs = xs + g_s[:, None, :] * y_ff[bp * tp:].reshape(bs, ts, d)
    y_prompt = rmsnorm(xp, norm_f)
    y_sample = rmsnorm(xs, norm_f)
    kv_rows_prompt = kv_p[:, :, :N_PAGED_SLOTS]
    kv_rows_sample = kv_s[:, :, :N_PAGED_SLOTS]
    win_prompt = kv_p[:, tp - min(WINDOW, tp):, N_PAGED_SLOTS:]
    win_sample = win_full_s[:, ts:]
    conv_prompt = jnp.stack(conv_p, axis=0)
    conv_sample = jnp.stack(conv_s, axis=0)
    return (y_prompt, y_sample, kv_rows_prompt, kv_rows_sample, win_prompt, win_sample, conv_prompt, conv_sample)
```

```python
import functools

import jax
import jax.numpy as jnp
import numpy as np
from jax import lax
from jax.experimental import pallas as pl
from jax.experimental.pallas import tpu as pltpu

F32 = jnp.float32
BF16 = jnp.bfloat16
I32 = jnp.int32

N_HEADS = 16
KV_HEADS = 2
GROUP = N_HEADS // KV_HEADS
HEAD_DIM = 128
ROT_DIM = HEAD_DIM // 4
ROPE_THETA = 500000.0
PAGE_SIZE = 128
CMP_LEN = 32
CMP_STRIDE = 16
SEL_LEN = 64
N_SEL = 16
WINDOW = 512
N_EXPERTS = 32
TOP_K = 4
SWIGLU_LIMIT = 7.0
SWIGLU_ALPHA = 1.702
EPS = 1e-6

V7X_VMEM_BYTES = 64 * 1024 * 1024
VMEM_LIMIT = 56 * 1024 * 1024
NEG = -1e30


def _cparams(sem):
    return pltpu.CompilerParams(dimension_semantics=sem, vmem_limit_bytes=VMEM_LIMIT)


def _bdot(a, b):
    return jnp.dot(a.astype(BF16), b.astype(BF16), preferred_element_type=F32)


def _bdot_t(a, b):
    return lax.dot_general(a.astype(BF16), b.astype(BF16), (((1,), (1,)), ((), ())),
                           preferred_element_type=F32)


def _ada_kernel(c_ref, w_ref, b_ref, o_ref):
    c = c_ref[...]
    a = c * jax.nn.sigmoid(c)
    o_ref[...] = _bdot(a, w_ref[...]) + b_ref[...]


def ada_matmul(c, w, b, tn=1024):
    g, d, n = w.shape
    r = c.shape[0]
    return pl.pallas_call(
        _ada_kernel,
        grid=(g, n // tn),
        in_specs=[pl.BlockSpec((r, d), lambda gi, j: (0, 0)),
                  pl.BlockSpec((None, d, tn), lambda gi, j: (gi, 0, j)),
                  pl.BlockSpec((None, 1, tn), lambda gi, j: (gi, 0, j))],
        out_specs=pl.BlockSpec((None, r, tn), lambda gi, j: (gi, 0, j)),
        out_shape=jax.ShapeDtypeStruct((g, r, n), F32),
        compiler_params=_cparams(("arbitrary", "arbitrary")),
        name="ada_matmul",
    )(c, w, b)


def _norm_body(x, g, sc, sh):
    ms = jnp.mean(x * x, axis=-1, keepdims=True)
    y = x * lax.rsqrt(ms + EPS) * g
    if sc is None:
        return y
    return y * (1.0 + sc) + sh


def _norm_kernel(x_ref, g_ref, sc_ref, sh_ref, o_ref):
    o_ref[...] = _norm_body(x_ref[...], g_ref[...], sc_ref[...], sh_ref[...]).astype(o_ref.dtype)


def _plain_norm_kernel(x_ref, g_ref, o_ref):
    o_ref[...] = _norm_body(x_ref[...], g_ref[...], None, None).astype(o_ref.dtype)


def _mod_specs(per_row, tm, d, rows_per_batch):
    if per_row:
        return pl.BlockSpec((tm, d), lambda i: (i, 0))
    tiles = rows_per_batch // tm
    return pl.BlockSpec((None, 1, d), lambda i: (i // tiles, 0, 0))


def norm_mod(x, g, sc, sh, *, rows_per_batch, per_row, out_dtype, tm):
    m, d = x.shape
    mod = _mod_specs(per_row, tm, d, rows_per_batch)
    return pl.pallas_call(
        _norm_kernel,
        grid=(m // tm,),
        in_specs=[pl.BlockSpec((tm, d), lambda i: (i, 0)),
                  pl.BlockSpec((1, d), lambda i: (0, 0)), mod, mod],
        out_specs=pl.BlockSpec((tm, d), lambda i: (i, 0)),
        out_shape=jax.ShapeDtypeStruct((m, d), out_dtype),
        compiler_params=_cparams(("arbitrary",)),
        name="norm_mod",
    )(x, g, sc, sh)


def plain_norm(x, g, tm):
    m, d = x.shape
    return pl.pallas_call(
        _plain_norm_kernel,
        grid=(m // tm,),
        in_specs=[pl.BlockSpec((tm, d), lambda i: (i, 0)), pl.BlockSpec((1, d), lambda i: (0, 0))],
        out_specs=pl.BlockSpec((tm, d), lambda i: (i, 0)),
        out_shape=jax.ShapeDtypeStruct((m, d), F32),
        compiler_params=_cparams(("arbitrary",)),
        name="plain_norm",
    )(x, g)


def _inproj_common(h_ref, wb_ref, wc_ref, wx_ref, wbf):
    @pl.when(pl.program_id(1) == 0)
    def _cast():
        wbf[0] = wb_ref[...].astype(BF16)
        wbf[1] = wc_ref[...].astype(BF16)
        wbf[2] = wx_ref[...].astype(BF16)

    h = h_ref[...]
    b = jnp.dot(h, wbf[0], preferred_element_type=F32)
    c = jnp.dot(h, wbf[1], preferred_element_type=F32)
    xt = jnp.dot(h, wbf[2], preferred_element_type=F32)
    return b * xt, c


def _inproj_seq_kernel(h_ref, wb_ref, wc_ref, wx_ref, wconv_ref, st_ref, v_ref, so_ref, wbf, carry, *, seq_tiles):
    i = pl.program_id(1)
    u, c = _inproj_common(h_ref, wb_ref, wc_ref, wx_ref, wbf)
    tm = u.shape[0]

    @pl.when(i % seq_tiles == 0)
    def _init():
        carry[0:2, :] = st_ref[...]

    prev0 = carry[0:1, :]
    prev1 = carry[1:2, :]
    rid = lax.broadcasted_iota(I32, u.shape, 0)
    u1 = jnp.where(rid == 0, prev1, pltpu.roll(u, 1, 0))
    u2 = jnp.where(rid == 0, prev0, jnp.where(rid == 1, prev1, pltpu.roll(u, 2, 0)))
    wc = wconv_ref[...]
    conv = wc[0:1, :] * u2 + wc[1:2, :] * u1 + wc[2:3, :] * u
    v_ref[...] = (c * conv).astype(v_ref.dtype)
    last = u[tm - 2:tm, :]
    carry[0:2, :] = last

    @pl.when(i % seq_tiles == seq_tiles - 1)
    def _state():
        so_ref[...] = last


def _inproj_rows_kernel(h_ref, wb_ref, wc_ref, wx_ref, wconv_ref, s1_ref, s2_ref, v_ref, u_ref, wbf, *, seq_len):
    u, c = _inproj_common(h_ref, wb_ref, wc_ref, wx_ref, wbf)
    r = lax.broadcasted_iota(I32, u.shape, 0) % seq_len
    u1 = jnp.where(r >= 1, pltpu.roll(u, 1, 0), s1_ref[...])
    u2 = jnp.where(r >= 2, pltpu.roll(u, 2, 0), s2_ref[...])
    wc = wconv_ref[...]
    conv = wc[0:1, :] * u2 + wc[1:2, :] * u1 + wc[2:3, :] * u
    v_ref[...] = (c * conv).astype(v_ref.dtype)
    u_ref[...] = u


def inproj_seq(h, w_in, w_conv, state, *, seq_len, tm, tn):
    m, k = h.shape
    d = w_in.shape[1] // 3
    nj = d // tn
    seq_tiles = seq_len // tm
    nb = m // seq_len
    wspec = [pl.BlockSpec((k, tn), functools.partial(lambda j, i, o: (0, o + j), o=o * nj)) for o in range(3)]
    return pl.pallas_call(
        functools.partial(_inproj_seq_kernel, seq_tiles=seq_tiles),
        grid=(nj, m // tm),
        in_specs=[pl.BlockSpec((tm, k), lambda j, i: (i, 0))] + wspec + [
            pl.BlockSpec((3, tn), lambda j, i: (0, j)),
            pl.BlockSpec((None, 2, tn), lambda j, i: (i // seq_tiles, 0, j))],
        out_specs=[pl.BlockSpec((tm, tn), lambda j, i: (i, j)),
                   pl.BlockSpec((None, 2, tn), lambda j, i: (i // seq_tiles, 0, j))],
        out_shape=[jax.ShapeDtypeStruct((m, d), BF16), jax.ShapeDtypeStruct((nb, 2, d), F32)],
        scratch_shapes=[pltpu.VMEM((3, k, tn), BF16), pltpu.VMEM((8, tn), F32)],
        compiler_params=_cparams(("arbitrary", "arbitrary")),
        name="inproj_seq",
    )(h, w_in, w_in, w_in, w_conv, state)


def inproj_rows(h, w_in, w_conv, s1, s2, *, seq_len, tn):
    m, k = h.shape
    d = w_in.shape[1] // 3
    nj = d // tn
    wspec = [pl.BlockSpec((k, tn), functools.partial(lambda j, i, o: (0, o + j), o=o * nj)) for o in range(3)]
    row = pl.BlockSpec((m, tn), lambda j, i: (0, j))
    return pl.pallas_call(
        functools.partial(_inproj_rows_kernel, seq_len=seq_len),
        grid=(nj, 1),
        in_specs=[pl.BlockSpec((m, k), lambda j, i: (0, 0))] + wspec + [
            pl.BlockSpec((3, tn), lambda j, i: (0, j)), row, row],
        out_specs=[row, row],
        out_shape=[jax.ShapeDtypeStruct((m, d), BF16), jax.ShapeDtypeStruct((m, d), F32)],
        scratch_shapes=[pltpu.VMEM((3, k, tn), BF16)],
        compiler_params=_cparams(("arbitrary", "arbitrary")),
        name="inproj_rows",
    )(h, w_in, w_in, w_in, w_conv, s1, s2)


def _resid_kernel(a_ref, w_ref, x_ref, g_ref, o_ref, wbf):
    @pl.when(pl.program_id(1) == 0)
    def _cast():
        wbf[...] = w_ref[...].astype(BF16)

    y = jnp.dot(a_ref[...], wbf[...], preferred_element_type=F32)
    o_ref[...] = x_ref[...] + g_ref[...] * y


def proj_residual(a, w, x, gate, *, rows_per_batch, per_row, tm, tn):
    m, k = a.shape
    n = w.shape[1]
    if per_row:
        gspec = pl.BlockSpec((tm, tn), lambda j, i: (i, j))
    else:
        tiles = rows_per_batch // tm
        gspec = pl.BlockSpec((None, 1, tn), lambda j, i: (i // tiles, 0, j))
    return pl.pallas_call(
        _resid_kernel,
        grid=(n // tn, m // tm),
        in_specs=[pl.BlockSpec((tm, k), lambda j, i: (i, 0)),
                  pl.BlockSpec((k, tn), lambda j, i: (0, j)),
                  pl.BlockSpec((tm, tn), lambda j, i: (i, j)), gspec],
        out_specs=pl.BlockSpec((tm, tn), lambda j, i: (i, j)),
        out_shape=jax.ShapeDtypeStruct((m, n), F32),
        scratch_shapes=[pltpu.VMEM((k, tn), BF16)],
        compiler_params=_cparams(("arbitrary", "arbitrary")),
        name="proj_residual",
    )(a, w, x, gate)


LANES = 128
MOE_SUB = 256
MOE_ITEM_SUBS = 5


def _route_kernel(x_ref, g_ref, sc_ref, sh_ref, wr_ref, br_ref, cin_ref,
                  h_ref, ei_ref, wt_ref, rk_ref, cout_ref, cnt):
    @pl.when(pl.program_id(0) == 0)
    def _init():
        cnt[...] = cin_ref[...]

    h = _norm_body(x_ref[...], g_ref[...], sc_ref[...], sh_ref[...])
    for c in range(h.shape[1] // LANES):
        h_ref[:, c, :] = h[:, c * LANES:(c + 1) * LANES]
    logits = jnp.dot(h, wr_ref[...], precision=lax.Precision.HIGHEST, preferred_element_type=F32) + br_ref[...]
    tm, ne = logits.shape
    lane = lax.broadcasted_iota(I32, (tm, ne), 1)
    cur = logits
    vals, idxs = [], []
    for _ in range(TOP_K):
        m = jnp.max(cur, axis=-1, keepdims=True)
        idx = jnp.min(jnp.where(cur == m, lane, ne), axis=-1, keepdims=True)
        vals.append(m)
        idxs.append(idx)
        cur = jnp.where(lane == idx, -jnp.inf, cur)
    es = [jnp.exp(v - vals[0]) for v in vals]
    den = es[0] + es[1] + es[2] + es[3]
    hot = [(lane == idx) for idx in idxs]
    multi = jnp.zeros((tm, ne), F32)
    for hk in hot:
        multi = multi + hk.astype(F32)
    tri = (lax.broadcasted_iota(I32, (tm, tm), 0) > lax.broadcasted_iota(I32, (tm, tm), 1)).astype(BF16)
    base = jnp.dot(tri, multi.astype(BF16), preferred_element_type=F32) + cnt[...]
    wide = lax.broadcasted_iota(I32, (tm, LANES), 1)
    ei = jnp.zeros((tm, LANES), I32)
    wt = jnp.zeros((tm, LANES), F32)
    rk = jnp.zeros((tm, LANES), F32)
    for k in range(TOP_K):
        rank_k = jnp.sum(jnp.where(hot[k], base, 0.0), axis=-1, keepdims=True)
        ei = jnp.where(wide == k, idxs[k], ei)
        wt = jnp.where(wide == k, es[k] / den, wt)
        rk = jnp.where(wide == k, rank_k, rk)
    ei_ref[...] = ei
    wt_ref[...] = wt
    rk_ref[...] = rk
    cnt[...] = cnt[...] + jnp.sum(multi, axis=0, keepdims=True)
    cout_ref[...] = cnt[...]


def route(x, g, sc, sh, w_router, b_router, counts_in, *, rows_per_batch, per_row, tm):
    m, d = x.shape
    ne = w_router.shape[1]
    mod = _mod_specs(per_row, tm, d, rows_per_batch)
    row = pl.BlockSpec((tm, LANES), lambda i: (i, 0))
    one = pl.BlockSpec((1, ne), lambda i: (0, 0))
    return pl.pallas_call(
        _route_kernel,
        grid=(m // tm,),
        in_specs=[pl.BlockSpec((tm, d), lambda i: (i, 0)), pl.BlockSpec((1, d), lambda i: (0, 0)), mod, mod,
                  pl.BlockSpec((d, ne), lambda i: (0, 0)), one, one],
        out_specs=[pl.BlockSpec((tm, d // LANES, LANES), lambda i: (i, 0, 0)), row, row, row, one],
        out_shape=[jax.ShapeDtypeStruct((m, d // LANES, LANES), F32), jax.ShapeDtypeStruct((m, LANES), I32),
                   jax.ShapeDtypeStruct((m, LANES), F32), jax.ShapeDtypeStruct((m, LANES), F32),
                   jax.ShapeDtypeStruct((1, ne), F32)],
        scratch_shapes=[pltpu.VMEM((1, ne), F32)],
        compiler_params=_cparams(("arbitrary",)),
        name="route",
    )(x, g, sc, sh, w_router, b_router, counts_in)


def moe_plan(ei, rk, counts, max_items):
    ne = counts.shape[0]
    item_rows = MOE_SUB * MOE_ITEM_SUBS
    padded = (counts + MOE_SUB - 1) // MOE_SUB * MOE_SUB
    pstart = jnp.cumsum(padded) - padded
    pos = pstart[ei] + rk
    zstart = jnp.where(counts % MOE_SUB != 0, pstart + counts // MOE_SUB * MOE_SUB, -1).astype(I32)
    n_chunks = (counts + item_rows - 1) // item_rows
    cum = jnp.cumsum(n_chunks)
    t = jnp.arange(max_items, dtype=I32)
    valid = t < cum[-1]
    e = jnp.minimum(jnp.searchsorted(cum, jnp.minimum(t, cum[-1] - 1), side="right"), ne - 1).astype(I32)
    c = t - (cum[e] - n_chunks[e])
    start = jnp.where(valid, pstart[e] + c * item_rows, 0).astype(I32)
    rows = jnp.where(valid, jnp.minimum(item_rows, counts[e] - c * item_rows), 0).astype(I32)
    return pos.astype(I32), zstart, e, start, rows


def _dispatch_kernel(zs_ref, pos_ref, h_hbm, *rest, tm, zero_fill):
    xs_hbm, zbuf, sem, zsem = rest[-4:]
    i = pl.program_id(0)
    if zero_fill:
        @pl.when(i == 0)
        def _zero():
            zbuf[...] = jnp.zeros(zbuf.shape, zbuf.dtype)
            for phase in range(2):
                for e in range(N_EXPERTS):
                    @pl.when(zs_ref[e] >= 0)
                    def _():
                        cp = pltpu.make_async_copy(zbuf, xs_hbm.at[pl.ds(zs_ref[e], MOE_SUB)], zsem)
                        if phase == 0:
                            cp.start()
                        else:
                            cp.wait()

    def row_copy(t, k):
        p = pos_ref[0, t * TOP_K + k]
        return pltpu.make_async_copy(h_hbm.at[i * tm + t], xs_hbm.at[p], sem)

    def start(t, carry):
        for k in range(TOP_K):
            row_copy(t, k).start()
        return carry

    def wait(t, carry):
        for k in range(TOP_K):
            row_copy(t, k).wait()
        return carry

    lax.fori_loop(0, tm, start, 0)
    lax.fori_loop(0, tm, wait, 0)


def dispatch(zstart, pos, h, xs, *, n_rows, tm):
    m, nc, _ = h.shape
    zero_fill = xs is None
    args = [zstart, pos.reshape(m // tm, 1, tm * TOP_K), h]
    in_specs = [pl.BlockSpec((None, 1, tm * TOP_K), lambda i, zs: (i, 0, 0), memory_space=pltpu.SMEM),
                pl.BlockSpec(memory_space=pl.ANY)]
    aliases = {}
    if not zero_fill:
        args.append(xs)
        in_specs.append(pl.BlockSpec(memory_space=pl.ANY))
        aliases = {3: 0}
    return pl.pallas_call(
        functools.partial(_dispatch_kernel, tm=tm, zero_fill=zero_fill),
        grid_spec=pltpu.PrefetchScalarGridSpec(
            num_scalar_prefetch=1, grid=(m // tm,), in_specs=in_specs,
            out_specs=pl.BlockSpec(memory_space=pl.ANY),
            scratch_shapes=[pltpu.VMEM((MOE_SUB, nc, LANES), F32), pltpu.SemaphoreType.DMA(()),
                            pltpu.SemaphoreType.DMA(())]),
        out_shape=jax.ShapeDtypeStruct((n_rows, nc, LANES), F32),
        input_output_aliases=aliases,
        compiler_params=_cparams(("arbitrary",)),
        name="moe_dispatch",
    )(*args)


def _expert_kernel(ie_ref, is_ref, ir_ref, xs_hbm, wg_ref, wu_ref, wd_ref, bg_ref, bu_ref, bd_ref, ys_hbm,
                   stage, acc, xbf, wgb, wub, wdb, sem_in, sem_out, *, nf):
    del ie_ref
    it = pl.program_id(0)
    f = pl.program_id(1)
    rows = ir_ref[it]
    start = is_ref[it]
    nsub = (rows + MOE_SUB - 1) // MOE_SUB
    nc = stage.shape[2]

    def copy_in(s):
        return pltpu.make_async_copy(xs_hbm.at[pl.ds(start + s * MOE_SUB, MOE_SUB)], stage.at[s], sem_in.at[s])

    def copy_out(s):
        return pltpu.make_async_copy(stage.at[s], ys_hbm.at[pl.ds(start + s * MOE_SUB, MOE_SUB)], sem_out.at[s])

    @pl.when(f == 0)
    def _load():
        for s in range(MOE_ITEM_SUBS):
            @pl.when(s < nsub)
            def _():
                copy_in(s).start()
        for s in range(MOE_ITEM_SUBS):
            @pl.when(s < nsub)
            def _():
                copy_in(s).wait()
                for c in range(nc):
                    xbf[s, :, c * LANES:(c + 1) * LANES] = stage[s, :, c, :].astype(BF16)

    @pl.when(rows > 0)
    def _compute():
        wgb[...] = wg_ref[...].astype(BF16)
        wub[...] = wu_ref[...].astype(BF16)
        wdb[...] = wd_ref[...].astype(BF16)
        for s in range(MOE_ITEM_SUBS):
            @pl.when(s < nsub)
            def _():
                x = xbf[s]
                gate = jnp.dot(x, wgb[...], preferred_element_type=F32) + bg_ref[...]
                up = jnp.dot(x, wub[...], preferred_element_type=F32) + bu_ref[...]
                gate = jnp.minimum(gate, SWIGLU_LIMIT)
                up = jnp.clip(up, -SWIGLU_LIMIT, SWIGLU_LIMIT)
                act = (up + 1.0) * (gate * jax.nn.sigmoid(SWIGLU_ALPHA * gate))
                y = jnp.dot(act.astype(BF16), wdb[...], preferred_element_type=F32)

                @pl.when(f == 0)
                def _():
                    acc[s] = y + bd_ref[...]

                @pl.when(f > 0)
                def _():
                    acc[s] = acc[s] + y

    @pl.when(f == nf - 1)
    def _store():
        for s in range(MOE_ITEM_SUBS):
            @pl.when(s < nsub)
            def _():
                for c in range(nc):
                    stage[s, :, c, :] = acc[s, :, c * LANES:(c + 1) * LANES]
                copy_out(s).start()
        for s in range(MOE_ITEM_SUBS):
            @pl.when(s < nsub)
            def _():
                copy_out(s).wait()


def experts(item_e, item_start, item_rows, xs, w_gu, b_gu, w_down, b_down, layer, *, tf):
    nc = xs.shape[1]
    d = nc * LANES
    d_ff = w_down.shape[2]
    nf = d_ff // tf
    n_items = item_e.shape[0]

    def fblk(it, f, ir):
        return jnp.where(ir[it] > 0, f, nf - 1)

    b_gu4 = b_gu.reshape(b_gu.shape[0], b_gu.shape[1], 1, 2 * d_ff)
    b_down4 = b_down.reshape(b_down.shape[0], b_down.shape[1], 1, d)
    return pl.pallas_call(
        functools.partial(_expert_kernel, nf=nf),
        grid_spec=pltpu.PrefetchScalarGridSpec(
            num_scalar_prefetch=3, grid=(n_items, nf),
            in_specs=[
                pl.BlockSpec(memory_space=pl.ANY),
                pl.BlockSpec((None, None, d, tf), lambda it, f, ie, is_, ir: (layer, ie[it], 0, fblk(it, f, ir))),
                pl.BlockSpec((None, None, d, tf), lambda it, f, ie, is_, ir: (layer, ie[it], 0, nf + fblk(it, f, ir))),
                pl.BlockSpec((None, None, tf, d), lambda it, f, ie, is_, ir: (layer, ie[it], fblk(it, f, ir), 0)),
                pl.BlockSpec((None, None, 1, tf), lambda it, f, ie, is_, ir: (layer, ie[it], 0, fblk(it, f, ir))),
                pl.BlockSpec((None, None, 1, tf), lambda it, f, ie, is_, ir: (layer, ie[it], 0, nf + fblk(it, f, ir))),
                pl.BlockSpec((None, None, 1, d), lambda it, f, ie, is_, ir: (layer, ie[it], 0, 0)),
            ],
            out_specs=pl.BlockSpec(memory_space=pl.ANY),
            scratch_shapes=[pltpu.VMEM((MOE_ITEM_SUBS, MOE_SUB, nc, LANES), F32),
                            pltpu.VMEM((MOE_ITEM_SUBS, MOE_SUB, d), F32),
                            pltpu.VMEM((MOE_ITEM_SUBS, MOE_SUB, d), BF16),
                            pltpu.VMEM((d, tf), BF16), pltpu.VMEM((d, tf), BF16), pltpu.VMEM((tf, d), BF16),
                            pltpu.SemaphoreType.DMA((MOE_ITEM_SUBS,)), pltpu.SemaphoreType.DMA((MOE_ITEM_SUBS,))]),
        out_shape=jax.ShapeDtypeStruct(xs.shape, F32),
        input_output_aliases={3: 0},
        compiler_params=_cparams(("arbitrary", "arbitrary")),
        name="moe_experts",
    )(item_e, item_start, item_rows, xs, w_gu, w_gu, w_down, b_gu4, b_gu4, b_down4)


def _combine_kernel(pos_ref, ys_hbm, wt_ref, x_ref, g_ref, o_ref, buf, sem, *, tm):
    def row_copy(t, k):
        p = pos_ref[0, t * TOP_K + k]
        return pltpu.make_async_copy(ys_hbm.at[p], buf.at[k, t], sem)

    def start(t, carry):
        for k in range(TOP_K):
            row_copy(t, k).start()
        return carry

    def wait(t, carry):
        for k in range(TOP_K):
            row_copy(t, k).wait()
        return carry

    lax.fori_loop(0, tm, start, 0)
    lax.fori_loop(0, tm, wait, 0)
    wt = wt_ref[...]
    for c in range(buf.shape[2]):
        y = wt[:, 0:1] * buf[0, :, c, :]
        for k in range(1, TOP_K):
            y = y + wt[:, k:k + 1] * buf[k, :, c, :]
        cols = slice(c * LANES, (c + 1) * LANES)
        o_ref[:, cols] = x_ref[:, cols] + g_ref[:, cols] * y


def combine(pos, ys, wt, x, gate, *, rows_per_batch, per_row, tm):
    m, d = x.shape
    nc = d // LANES
    if per_row:
        gspec = pl.BlockSpec((tm, d), lambda i: (i, 0))
    else:
        tiles = rows_per_batch // tm
        gspec = pl.BlockSpec((None, 1, d), lambda i: (i // tiles, 0, 0))
    return pl.pallas_call(
        functools.partial(_combine_kernel, tm=tm),
        grid=(m // tm,),
        in_specs=[pl.BlockSpec((None, 1, tm * TOP_K), lambda i: (i, 0, 0), memory_space=pltpu.SMEM),
                  pl.BlockSpec(memory_space=pl.ANY),
                  pl.BlockSpec((tm, LANES), lambda i: (i, 0)),
                  pl.BlockSpec((tm, d), lambda i: (i, 0)), gspec],
        out_specs=pl.BlockSpec((tm, d), lambda i: (i, 0)),
        out_shape=jax.ShapeDtypeStruct((m, d), F32),
        scratch_shapes=[pltpu.VMEM((TOP_K, tm, nc, LANES), F32), pltpu.SemaphoreType.DMA(())],
        compiler_params=_cparams(("arbitrary",)),
        name="moe_combine",
    )(pos.reshape(m // tm, 1, tm * TOP_K), ys, wt, x, gate)


def rotary_tables(pos):
    half = ROT_DIM // 2
    inv = ROPE_THETA ** (-jnp.arange(0, ROT_DIM, 2, dtype=F32) / ROT_DIM)
    ang = pos.astype(F32)[:, None] * inv[None, :]
    cos, sin = jnp.cos(ang), jnp.sin(ang)
    n = pos.shape[0]
    rest = HEAD_DIM - ROT_DIM
    c = jnp.concatenate([cos, cos, jnp.ones((n, rest), F32)], axis=1)
    a = jnp.concatenate([-sin, jnp.zeros((n, HEAD_DIM - half), F32)], axis=1)
    b = jnp.concatenate([jnp.zeros((n, half), F32), sin, jnp.zeros((n, rest), F32)], axis=1)
    return c, a, b


def _rot(seg, c, a, b):
    half = ROT_DIM // 2
    return seg * c + pltpu.roll(seg, HEAD_DIM - half, 1) * a + pltpu.roll(seg, half, 1) * b


def _kv_kernel(h_ref, w_ref, c_ref, a_ref, b_ref, o_ref, *rest, head_major):
    if head_major:
        ob_ref, wbf = rest
    else:
        (wbf,) = rest

    @pl.when(pl.program_id(1) == 0)
    def _cast():
        wbf[...] = w_ref[...].astype(BF16)

    y = jnp.dot(h_ref[...], wbf[...], preferred_element_type=F32)
    c, a, b = c_ref[...], a_ref[...], b_ref[...]
    for s in range(2 * KV_HEADS):
        seg = y[:, s * HEAD_DIM:(s + 1) * HEAD_DIM]
        if s < KV_HEADS:
            seg = _rot(seg, c, a, b)
        o_ref[:, s * HEAD_DIM:(s + 1) * HEAD_DIM] = seg
        if head_major:
            ob_ref[s] = seg.astype(BF16)


def kv_proj(h, w_kv, tabs, *, seq_len, tm, head_major):
    m, k = h.shape
    n = w_kv.shape[1]
    tn = 2 * KV_HEADS * HEAD_DIM
    seq_tiles = seq_len // tm if head_major else 1
    tab = pl.BlockSpec((tm, HEAD_DIM), lambda j, i: (i % seq_tiles, 0))
    out_specs = [pl.BlockSpec((tm, tn), lambda j, i: (i, j))]
    out_shape = [jax.ShapeDtypeStruct((m, n), F32)]
    if head_major:
        out_specs.append(pl.BlockSpec((None, 2 * KV_HEADS, tm, HEAD_DIM),
                                      lambda j, i: (i // seq_tiles, j, i % seq_tiles, 0)))
        out_shape.append(jax.ShapeDtypeStruct((m // seq_len, n // HEAD_DIM, seq_len, HEAD_DIM), BF16))
    return pl.pallas_call(
        functools.partial(_kv_kernel, head_major=head_major),
        grid=(n // tn, m // tm),
        in_specs=[pl.BlockSpec((tm, k), lambda j, i: (i, 0)), pl.BlockSpec((k, tn), lambda j, i: (0, j)),
                  tab, tab, tab],
        out_specs=out_specs, out_shape=out_shape,
        scratch_shapes=[pltpu.VMEM((k, tn), BF16)],
        compiler_params=_cparams(("arbitrary", "arbitrary")),
        name="kv_proj",
    )(h, w_kv, *tabs)


def _q_kernel(h_ref, w_ref, c_ref, a_ref, b_ref, o_ref, wbf, *, scale):
    @pl.when(pl.program_id(1) == 0)
    def _cast():
        wbf[...] = w_ref[...].astype(BF16)

    y = jnp.dot(h_ref[...], wbf[...], preferred_element_type=F32)
    c, a, b = c_ref[...], a_ref[...], b_ref[...]
    for s in range(y.shape[1] // HEAD_DIM):
        seg = _rot(y[:, s * HEAD_DIM:(s + 1) * HEAD_DIM], c, a, b)
        o_ref[:, s * HEAD_DIM:(s + 1) * HEAD_DIM] = (seg * scale).astype(o_ref.dtype)


def q_proj(h, w_qg, tabs, *, seq_len, tm, tn, per_row):
    m, k = h.shape
    n = N_HEADS * HEAD_DIM
    seq_tiles = 1 if per_row else seq_len // tm
    tab = pl.BlockSpec((tm, HEAD_DIM), lambda j, i: (i % seq_tiles, 0))
    return pl.pallas_call(
        functools.partial(_q_kernel, scale=HEAD_DIM ** -0.5),
        grid=(n // tn, m // tm),
        in_specs=[pl.BlockSpec((tm, k), lambda j, i: (i, 0)), pl.BlockSpec((k, tn), lambda j, i: (0, j)),
                  tab, tab, tab],
        out_specs=pl.BlockSpec((tm, tn), lambda j, i: (i, j)),
        out_shape=jax.ShapeDtypeStruct((m, n), BF16),
        scratch_shapes=[pltpu.VMEM((k, tn), BF16)],
        compiler_params=_cparams(("arbitrary", "arbitrary")),
        name="q_proj",
    )(h, w_qg, *tabs)


def _gate_kernel(h_ref, w_ref, o_ref):
    o_ref[...] = jax.nn.sigmoid(_bdot(h_ref[...], w_ref[...]))


def gate_proj(h, w_g, *, tm):
    m, k = h.shape
    return pl.pallas_call(
        _gate_kernel,
        grid=(m // tm,),
        in_specs=[pl.BlockSpec((tm, k), lambda i: (i, 0)), pl.BlockSpec((k, LANES), lambda i: (0, 0))],
        out_specs=pl.BlockSpec((tm, LANES), lambda i: (i, 0)),
        out_shape=jax.ShapeDtypeStruct((m, LANES), F32),
        compiler_params=_cparams(("arbitrary",)),
        name="gate_proj",
    )(h, w_g)


CMP_PAGES = 16
CHUNKS_PER_PAGE = PAGE_SIZE // CMP_STRIDE


def _compress_kernel(pt_ref, *refs, n_steps):
    del pt_ref
    pages = refs[:CMP_PAGES]
    w1_ref, pe_ref, w2_ref, o_ref, p_sc, head_sc = refs[CMP_PAGES:]
    j = pl.program_id(1)
    rows = CMP_PAGES * CHUNKS_PER_PAGE
    flat = CMP_STRIDE * HEAD_DIM
    n_slots = 2 * KV_HEADS
    for p, pg in enumerate(pages):
        for c in range(n_slots):
            head_sc[p * n_slots + c] = pg[:, c * HEAD_DIM:(c + 1) * HEAD_DIM]
    for kv in range(2):
        xs = []
        for g in range(KV_HEADS):
            per_page = []
            for p in range(CMP_PAGES):
                slot = p * n_slots + kv * KV_HEADS + g
                pieces = [head_sc[slot, pl.ds(s, CHUNKS_PER_PAGE, stride=CMP_STRIDE), :] for s in range(CMP_STRIDE)]
                per_page.append(jnp.concatenate(pieces, axis=1))
            xs.append(jnp.concatenate(per_page, axis=0))
        x = jnp.concatenate(xs, axis=0)
        for r in range(CMP_LEN // CMP_STRIDE):
            xr = x + pe_ref[kv, :, r * flat:(r + 1) * flat]
            pr = _bdot(xr, w1_ref[kv, r * flat:(r + 1) * flat, :])
            for g in range(KV_HEADS):
                p_sc[r, kv * KV_HEADS + g, pl.ds(j * rows, rows), :] = pr[g * rows:(g + 1) * rows]

    @pl.when(j == n_steps - 1)
    def _finish():
        n = p_sc.shape[2]
        for kv in range(2):
            for g in range(KV_HEADS):
                c = kv * KV_HEADS + g
                hid = p_sc[0, c] + pltpu.roll(p_sc[1, c], n - 1, 0)
                act = hid * jax.nn.sigmoid(hid)
                o_ref[c] = _bdot(act, w2_ref[kv]).astype(o_ref.dtype)


def compress(pages, page_table, w_cmp1, w_cmp2, pe_cmp):
    nb, n_pages = page_table.shape
    n_steps = n_pages // CMP_PAGES
    n_chunks = n_pages * CHUNKS_PER_PAGE
    width = 2 * KV_HEADS * HEAD_DIM
    flat = CMP_LEN * HEAD_DIM
    page_specs = [pl.BlockSpec((None, PAGE_SIZE, width),
                               functools.partial(lambda b, j, pt, p: (pt[b, j * CMP_PAGES + p], 0, 0), p=p))
                  for p in range(CMP_PAGES)]
    return pl.pallas_call(
        functools.partial(_compress_kernel, n_steps=n_steps),
        grid_spec=pltpu.PrefetchScalarGridSpec(
            num_scalar_prefetch=1, grid=(nb, n_steps),
            in_specs=page_specs + [
                pl.BlockSpec((2, flat, HEAD_DIM), lambda b, j, pt: (0, 0, 0)),
                pl.BlockSpec((2, 1, flat), lambda b, j, pt: (0, 0, 0)),
                pl.BlockSpec((2, HEAD_DIM, HEAD_DIM), lambda b, j, pt: (0, 0, 0))],
            out_specs=pl.BlockSpec((None, 2 * KV_HEADS, n_chunks, HEAD_DIM), lambda b, j, pt: (b, 0, 0, 0)),
            scratch_shapes=[pltpu.VMEM((2, 2 * KV_HEADS, n_chunks, HEAD_DIM), F32),
                            pltpu.VMEM((CMP_PAGES * 2 * KV_HEADS, PAGE_SIZE, HEAD_DIM), F32)]),
        out_shape=jax.ShapeDtypeStruct((nb, 2 * KV_HEADS, n_chunks, HEAD_DIM), BF16),
        compiler_params=_cparams(("arbitrary", "arbitrary")),
        name="compress",
    )(page_table, *([pages] * CMP_PAGES), w_cmp1.reshape(2, flat, HEAD_DIM), pe_cmp.reshape(2, 1, flat), w_cmp2)


def _softmax_rows(s, mask):
    sm = jnp.where(mask, s, NEG)
    m = jnp.max(sm, axis=-1, keepdims=True)
    p = jnp.where(mask, jnp.exp(sm - m), 0.0)
    den = jnp.sum(p, axis=-1, keepdims=True)
    return p / jnp.where(den > 0, den, 1.0)


def _overlap(n_cmp_blocks, n_sel_blocks, transposed=False):
    shape = (n_sel_blocks, n_cmp_blocks) if transposed else (n_cmp_blocks, n_sel_blocks)
    ci = lax.broadcasted_iota(I32, shape, 1 if transposed else 0) * CMP_STRIDE
    sj = lax.broadcasted_iota(I32, shape, 0 if transposed else 1) * SEL_LEN
    return ((ci <= sj + SEL_LEN - 1) & (ci + CMP_LEN - 1 >= sj)).astype(F32)


def _online_update(s, allow, v, m_sc, l_sc, acc_sc):
    sm = jnp.where(allow, s, NEG)
    m_old = m_sc[...]
    m_new = jnp.maximum(m_old, jnp.max(sm, axis=-1, keepdims=True))
    alpha = jnp.exp(m_old - m_new)
    p = jnp.exp(sm - m_new)
    l_sc[...] = alpha * l_sc[...] + jnp.sum(p, axis=-1, keepdims=True)
    acc_sc[...] = alpha * acc_sc[...] + _bdot(p, v)
    m_sc[...] = m_new


def _reset(m_sc, l_sc, acc_sc):
    m_sc[...] = jnp.full(m_sc.shape, NEG, F32)
    l_sc[...] = jnp.zeros(l_sc.shape, F32)
    acc_sc[...] = jnp.zeros(acc_sc.shape, F32)


ATT_CK = 128


def _nsa_prompt_kernel(q_ref, gt_ref, slc_ref, win_ref, cmp_ref, o_ref, selx, m_sc, l_sc, acc_sc, *, tq, n_cmp, seq_len):
    i = pl.program_id(1)
    s0 = i * tq
    rows = GROUP * tq
    n_chunks = cmp_ref.shape[1]
    nb = seq_len // SEL_LEN
    tpos = s0 + lax.broadcasted_iota(I32, (tq, 1), 0)
    tpos_h = s0 + lax.broadcasted_iota(I32, (rows, 1), 0) % tq
    gates = gt_ref[...]
    for g in range(KV_HEADS):
        qg = jnp.concatenate([q_ref[:, (g * GROUP + h) * HEAD_DIM:(g * GROUP + h + 1) * HEAD_DIM]
                              for h in range(GROUP)], axis=0)
        n_id = lax.broadcasted_iota(I32, (1, n_chunks), 1)
        mask_c = (n_id * CMP_STRIDE + CMP_LEN - 1 <= tpos_h) & (n_id < n_cmp)
        p_c = _softmax_rows(_bdot_t(qg, cmp_ref[g]), mask_c)
        o_c = _bdot(p_c, cmp_ref[KV_HEADS + g])
        psum = p_c[0:tq]
        for h in range(1, GROUP):
            psum = psum + p_c[h * tq:(h + 1) * tq]
        imp = jnp.dot(psum, _overlap(n_chunks, nb), precision=lax.Precision.HIGHEST, preferred_element_type=F32)
        jj = lax.broadcasted_iota(I32, (tq, nb), 1)
        bt = tpos // SEL_LEN
        forced = (jj == 0) | (jj == bt) | (jj == bt - 1)
        key = jnp.where(forced, jnp.inf, jnp.where(jj <= bt, imp, -jnp.inf))
        rank = jnp.zeros((tq, nb), F32)
        for c in range(nb):
            col = key[:, c:c + 1]
            ahead = (col > key) | ((col == key) & (jj > c))
            rank = rank + ahead.astype(F32)
        sel = ((rank < N_SEL) & (jj <= bt)).astype(BF16)
        expand = (lax.broadcasted_iota(I32, (nb, seq_len), 1) // SEL_LEN
                  == lax.broadcasted_iota(I32, (nb, seq_len), 0)).astype(BF16)
        selk = jnp.dot(sel, expand, preferred_element_type=F32)
        for c in range(seq_len // ATT_CK):
            selx[c] = selk[:, c * ATT_CK:(c + 1) * ATT_CK]

        _reset(m_sc, l_sc, acc_sc)

        def sel_step(c, carry):
            off = pl.multiple_of(c * ATT_CK, ATT_CK)
            kpos = off + lax.broadcasted_iota(I32, (1, ATT_CK), 1)
            allow = (selx[c] > 0.5) & (kpos <= tpos)
            allow = jnp.concatenate([allow] * GROUP, axis=0)
            s = _bdot_t(qg, slc_ref[g, pl.ds(off, ATT_CK), :])
            _online_update(s, allow, slc_ref[KV_HEADS + g, pl.ds(off, ATT_CK), :], m_sc, l_sc, acc_sc)
            return carry

        lax.fori_loop(0, (s0 + tq) // ATT_CK, sel_step, 0)
        o_s = acc_sc[...] / l_sc[...]

        _reset(m_sc, l_sc, acc_sc)

        def win_step(c, carry):
            off = pl.multiple_of(c * ATT_CK, ATT_CK)
            kpos = off + lax.broadcasted_iota(I32, (1, ATT_CK), 1)
            allow = (kpos <= tpos_h) & (tpos_h - kpos < WINDOW)
            s = _bdot_t(qg, win_ref[g, pl.ds(off, ATT_CK), :])
            _online_update(s, allow, win_ref[KV_HEADS + g, pl.ds(off, ATT_CK), :], m_sc, l_sc, acc_sc)
            return carry

        lax.fori_loop(jnp.maximum(s0 - WINDOW, 0) // ATT_CK, (s0 + tq) // ATT_CK, win_step, 0)
        o_w = acc_sc[...] / l_sc[...]

        for h in range(GROUP):
            hh = g * GROUP + h
            r0 = h * tq
            o_h = (gates[:, 3 * hh:3 * hh + 1] * o_c[r0:r0 + tq]
                   + gates[:, 3 * hh + 1:3 * hh + 2] * o_s[r0:r0 + tq]
                   + gates[:, 3 * hh + 2:3 * hh + 3] * o_w[r0:r0 + tq])
            o_ref[:, hh * HEAD_DIM:(hh + 1) * HEAD_DIM] = o_h.astype(o_ref.dtype)


def nsa_prompt(q, gates, kvb, cmp, *, seq_len, tq):
    m, hd = q.shape
    nbatch = m // seq_len
    tiles = seq_len // tq
    n_chunks = cmp.shape[2]
    n_cmp = seq_len // CMP_STRIDE - CMP_LEN // CMP_STRIDE + 1
    rows = GROUP * tq
    blk = 2 * KV_HEADS
    return pl.pallas_call(
        functools.partial(_nsa_prompt_kernel, tq=tq, n_cmp=n_cmp, seq_len=seq_len),
        grid=(nbatch, tiles),
        in_specs=[pl.BlockSpec((tq, hd), lambda b, i: (b * tiles + i, 0)),
                  pl.BlockSpec((tq, LANES), lambda b, i: (b * tiles + i, 0)),
                  pl.BlockSpec((None, blk, seq_len, HEAD_DIM), lambda b, i: (b, 1, 0, 0)),
                  pl.BlockSpec((None, blk, seq_len, HEAD_DIM), lambda b, i: (b, 2, 0, 0)),
                  pl.BlockSpec((None, blk, n_chunks, HEAD_DIM), lambda b, i: (b, 0, 0, 0))],
        out_specs=pl.BlockSpec((tq, hd), lambda b, i: (b * tiles + i, 0)),
        out_shape=jax.ShapeDtypeStruct((m, hd), BF16),
        scratch_shapes=[pltpu.VMEM((seq_len // ATT_CK, tq, ATT_CK), F32),
                        pltpu.VMEM((rows, 1), F32), pltpu.VMEM((rows, 1), F32), pltpu.VMEM((rows, HEAD_DIM), F32)],
        compiler_params=_cparams(("arbitrary", "arbitrary")),
        name="nsa_prompt",
    )(q, gates, kvb, kvb, cmp)


SMP_PAGES = 8
SEL_PAD = 384
WIN_PAD = 640


def _nsa_sample_kernel(pt_ref, q_ref, gt_ref, cmp_ref, *refs, past_len, n_tok, n_buf, n_cmp):
    del pt_ref
    pages = refs[:SMP_PAGES]
    new_ref, win_ref, o_ref, selx, oc_sc, m_sc, l_sc, acc_sc = refs[SMP_PAGES:]
    j = pl.program_id(1)
    n_steps = pl.num_programs(1)
    rows = n_tok * GROUP
    n_chunks = cmp_ref.shape[1]
    n_pages = past_len // PAGE_SIZE
    nb = -(-(past_len + n_tok) // SEL_LEN)
    tpos_r = past_len + lax.broadcasted_iota(I32, (rows, 1), 0) // GROUP
    tpos_q = past_len + lax.broadcasted_iota(I32, (n_tok, 1), 0)

    @pl.when(j == 0)
    def _select():
        sels = []
        for g in range(KV_HEADS):
            qg = q_ref[g]
            n_id = lax.broadcasted_iota(I32, (1, n_chunks), 1)
            mask_c = (n_id * CMP_STRIDE + CMP_LEN - 1 <= tpos_r) & (n_id < n_cmp)
            p_c = _softmax_rows(_bdot_t(qg, cmp_ref[g]), mask_c)
            oc_sc[g] = _bdot(p_c, cmp_ref[KV_HEADS + g])
            psum = jnp.sum(p_c.reshape(n_tok, GROUP, n_chunks), axis=1)
            imp = jnp.dot(psum, _overlap(n_chunks, SEL_PAD), precision=lax.Precision.HIGHEST,
                          preferred_element_type=F32)
            jj = lax.broadcasted_iota(I32, (n_tok, SEL_PAD), 1)
            bt = tpos_q // SEL_LEN
            forced = (jj == 0) | (jj == bt) | (jj == bt - 1)
            key = jnp.where(jj < nb, jnp.where(forced, jnp.inf, jnp.where(jj <= bt, imp, -jnp.inf)), -jnp.inf)
            alive = jj >= 0
            for _ in range(N_SEL):
                cur = jnp.where(alive, key, -jnp.inf)
                mx = jnp.max(cur, axis=-1, keepdims=True)
                idx = jnp.min(jnp.where(alive & (cur == mx), jj, SEL_PAD), axis=-1, keepdims=True)
                alive = alive & (jj != idx)
            sels.append(((~alive) & (jj <= bt) & (jj < nb)).astype(F32))
        sel = jnp.concatenate(sels + [jnp.zeros((8 - KV_HEADS * n_tok, SEL_PAD), F32)] * (KV_HEADS * n_tok < 8),
                              axis=0)

        def fill(pg, carry):
            blk = lax.broadcasted_iota(I32, (SEL_PAD, PAGE_SIZE), 0)
            kk = lax.broadcasted_iota(I32, (SEL_PAD, PAGE_SIZE), 1)
            expand = (blk == pg * (PAGE_SIZE // SEL_LEN) + kk // SEL_LEN).astype(F32)
            e = jnp.dot(sel, expand, preferred_element_type=F32)
            for g in range(KV_HEADS):
                selx[pg, g] = jnp.concatenate(
                    [jnp.broadcast_to(e[g * n_tok + q:g * n_tok + q + 1], (GROUP, PAGE_SIZE)) for q in range(n_tok)],
                    axis=0)
            return carry

        lax.fori_loop(0, n_pages + 1, fill, 0)
        _reset(m_sc, l_sc, acc_sc)

    def process(page_ref, pg):
        kpos = pg * PAGE_SIZE + lax.broadcasted_iota(I32, (1, PAGE_SIZE), 1)
        for g in range(KV_HEADS):
            k = page_ref[:, g * HEAD_DIM:(g + 1) * HEAD_DIM]
            v = page_ref[:, (KV_HEADS + g) * HEAD_DIM:(KV_HEADS + g + 1) * HEAD_DIM]
            allow = (selx[pg, g] > 0.5) & (kpos <= tpos_r)
            _online_update(_bdot_t(q_ref[g], k), allow, v, m_sc.at[g], l_sc.at[g], acc_sc.at[g])

    @pl.when(j < n_steps - 1)
    def _pages():
        for p in range(SMP_PAGES):
            process(pages[p], j * SMP_PAGES + p)

    @pl.when(j == n_steps - 1)
    def _finish():
        process(new_ref, n_pages)
        idx = lax.broadcasted_iota(I32, (1, WIN_PAD), 1)
        dt = tpos_r - (past_len - n_buf + idx)
        allow_w = (dt >= 0) & (dt < WINDOW) & (idx < n_buf + n_tok)
        for g in range(KV_HEADS):
            kw = win_ref[:, g * HEAD_DIM:(g + 1) * HEAD_DIM]
            vw = win_ref[:, (KV_HEADS + g) * HEAD_DIM:(KV_HEADS + g + 1) * HEAD_DIM]
            p_w = _softmax_rows(_bdot_t(q_ref[g], kw), allow_w)
            o_w = _bdot(p_w, vw)
            o_s = acc_sc[g] / l_sc[g]
            gt = gt_ref[g]
            o_ref[g] = gt[:, 0:1] * oc_sc[g] + gt[:, 1:2] * o_s + gt[:, 2:3] * o_w


def nsa_sample(q, gates, cmp, pages, page_table, new_page, win_full, *, n_tok, n_buf):
    nb, n_pages = page_table.shape
    past_len = n_pages * PAGE_SIZE
    rows = n_tok * GROUP
    n_chunks = cmp.shape[2]
    n_cmp = (past_len + n_tok) // CMP_STRIDE - CMP_LEN // CMP_STRIDE + 1
    width = 2 * KV_HEADS * HEAD_DIM
    n_steps = n_pages // SMP_PAGES + 1
    blk = pl.BlockSpec((None, KV_HEADS, rows, HEAD_DIM), lambda b, j, pt: (b, 0, 0, 0))
    page_specs = [pl.BlockSpec(
        (None, PAGE_SIZE, width),
        functools.partial(lambda b, j, pt, p: (pt[b, jnp.minimum(j * SMP_PAGES + p, n_pages - 1)], 0, 1), p=p))
        for p in range(SMP_PAGES)]
    return pl.pallas_call(
        functools.partial(_nsa_sample_kernel, past_len=past_len, n_tok=n_tok, n_buf=n_buf, n_cmp=n_cmp),
        grid_spec=pltpu.PrefetchScalarGridSpec(
            num_scalar_prefetch=1, grid=(nb, n_steps),
            in_specs=[blk, blk, pl.BlockSpec((None, 2 * KV_HEADS, n_chunks, HEAD_DIM), lambda b, j, pt: (b, 0, 0, 0))]
            + page_specs + [pl.BlockSpec((None, PAGE_SIZE, width), lambda b, j, pt: (b, 0, 0)),
                            pl.BlockSpec((None, WIN_PAD, width), lambda b, j, pt: (b, 0, 0))],
            out_specs=blk,
            scratch_shapes=[pltpu.VMEM((n_pages + 1, KV_HEADS, rows, PAGE_SIZE), F32),
                            pltpu.VMEM((KV_HEADS, rows, HEAD_DIM), F32),
                            pltpu.VMEM((KV_HEADS, rows, 1), F32), pltpu.VMEM((KV_HEADS, rows, 1), F32),
                            pltpu.VMEM((KV_HEADS, rows, HEAD_DIM), F32)]),
        out_shape=jax.ShapeDtypeStruct((nb, KV_HEADS, rows, HEAD_DIM), F32),
        compiler_params=_cparams(("arbitrary", "arbitrary")),
        name="nsa_sample",
    )(page_table, q, gates, cmp, *([pages] * SMP_PAGES), new_page, win_full)


ROW_TILE = 512
COL_TILE = 512
ADA_ROWS = 16


class _Group:
    def __init__(self, vec, n_prompt, n_sample, n_tok):
        self.vec, self.n_prompt, self.n_sample, self.n_tok = vec, n_prompt, n_sample, n_tok

    def prompt(self, k, n):
        d = self.vec.shape[-1] // n
        return self.vec[:self.n_prompt, None, k * d:(k + 1) * d]

    def sample(self, k, n):
        d = self.vec.shape[-1] // n
        v = self.vec[self.n_prompt:self.n_prompt + self.n_sample, k * d:(k + 1) * d]
        return jnp.repeat(v, self.n_tok, axis=0)


def _moe_block(xp, xs, g_norm, mod, layer, tp, ts, w_router, b_router, w_gu, b_gu, w_down, b_down):
    mp, d = xp.shape
    ms = xs.shape[0]
    ne = w_router.shape[-1]
    wr, br = w_router[layer], b_router[layer][None]
    cin = jnp.zeros((1, ne), F32)
    hp, eip, wtp, rkp, c1 = route(xp, g_norm, mod.prompt(1, 3), mod.prompt(0, 3), wr, br, cin,
                                  rows_per_batch=tp, per_row=False, tm=256)
    hs, eis, wts, rks, c2 = route(xs, g_norm, mod.sample(1, 3), mod.sample(0, 3), wr, br, c1,
                                  rows_per_batch=ts, per_row=True, tm=ms)
    counts = c2[0].astype(I32)
    ei = jnp.concatenate([eip, eis])[:, :TOP_K]
    rk = jnp.concatenate([rkp, rks])[:, :TOP_K].astype(I32)
    nk = (mp + ms) * TOP_K
    n_rows = nk + ne * MOE_SUB
    max_items = ne + -(-nk // (MOE_SUB * MOE_ITEM_SUBS))
    pos, zstart, item_e, item_start, item_rows = moe_plan(ei, rk, counts, max_items)
    buf = dispatch(zstart, pos[:mp], hp, None, n_rows=n_rows, tm=256)
    buf = dispatch(zstart, pos[mp:], hs, buf, n_rows=n_rows, tm=ms)
    buf = experts(item_e, item_start, item_rows, buf, w_gu, b_gu, w_down, b_down, layer, tf=256)
    xp = combine(pos[:mp], buf, wtp, xp, mod.prompt(2, 3), rows_per_batch=tp, per_row=False, tm=128)
    xs = combine(pos[mp:], buf, wts, xs, mod.sample(2, 3), rows_per_batch=ts, per_row=True, tm=ms)
    return xp, xs


def kernel(x_prompt, x_sample, cache_kv, cache_win, state_conv, page_table, c_prompt, c_sample, norm_g, w_ada, b_ada,
           w_a_in, w_a_conv, w_a_out, norm_kv, w_ada_kv, b_ada_kv, w_kv, w_cmp1, w_cmp2, pe_cmp, w_b_qg, w_b_o,
           w_router, b_router, w_gu, b_gu, w_down, b_down, norm_f):
    bp, tp, d = x_prompt.shape
    bs, ts, _ = x_sample.shape
    mp, ms = bp * tp, bs * ts
    depth = w_ada.shape[0]
    n_pages = page_table.shape[1]
    past_len = n_pages * PAGE_SIZE
    n_buf = cache_win.shape[1]
    hd = N_HEADS * HEAD_DIM
    kvw = 2 * KV_HEADS * HEAD_DIM

    c_all = jnp.concatenate([c_prompt, c_sample, jnp.zeros((ADA_ROWS - bp - bs, d), F32)], axis=0)
    mods = ada_matmul(c_all, w_ada.reshape(depth * 2, d, 3 * d), b_ada.reshape(depth * 2, 1, 3 * d))
    mod_kv = _Group(ada_matmul(c_all, w_ada_kv[None], b_ada_kv[None, None])[0], bp, bs, ts)
    mod = [[_Group(mods[2 * layer + s], bp, bs, ts) for s in range(2)] for layer in range(depth)]

    def norm_pair(xp, xs, g, m, n, dtype):
        hp = norm_mod(xp, g[None], m.prompt(1, n), m.prompt(0, n), rows_per_batch=tp, per_row=False,
                      out_dtype=dtype, tm=ROW_TILE)
        hs = norm_mod(xs, g[None], m.sample(1, n), m.sample(0, n), rows_per_batch=ts, per_row=True,
                      out_dtype=dtype, tm=ms)
        return hp, hs

    xp = x_prompt.reshape(mp, d)
    xs = x_sample.reshape(ms, d)

    hp, hs = norm_pair(xp, xs, norm_g[0, 0], mod[0][0], 3, BF16)
    vp, conv_p = inproj_seq(hp, w_a_in[0], w_a_conv[0], jnp.zeros((bp, 2, d), F32), seq_len=tp, tm=ROW_TILE,
                            tn=COL_TILE)
    st = state_conv[0]
    s1 = jnp.repeat(st[:, 1], ts, axis=0)
    s2 = jnp.concatenate([st, jnp.zeros((bs, ts - 2, d), F32)], axis=1).reshape(ms, d)
    vs, us = inproj_rows(hs, w_a_in[0], w_a_conv[0], s1, s2, seq_len=ts, tn=COL_TILE)
    conv_s = us.reshape(bs, ts, d)[:, ts - 2:]
    xp = proj_residual(vp, w_a_out[0], xp, mod[0][0].prompt(2, 3), rows_per_batch=tp, per_row=False, tm=ROW_TILE,
                       tn=COL_TILE)
    xs = proj_residual(vs, w_a_out[0], xs, mod[0][0].sample(2, 3), rows_per_batch=ts, per_row=True, tm=ms,
                       tn=COL_TILE)
    xp, xs = _moe_block(xp, xs, norm_g[0, 1][None], mod[0][1], 0, tp, ts, w_router, b_router, w_gu, b_gu, w_down,
                        b_down)

    hp, hs = norm_pair(xp, xs, norm_kv, mod_kv, 2, BF16)
    tabs_p = rotary_tables(jnp.arange(tp))
    tabs_s = rotary_tables(past_len + jnp.arange(ms) % ts)
    kv_p, kvb_p = kv_proj(hp, w_kv, tabs_p, seq_len=tp, tm=ROW_TILE, head_major=True)
    (kv_s,) = kv_proj(hs, w_kv, tabs_s, seq_len=ts, tm=ms, head_major=False)
    own_pages = jnp.arange(mp // PAGE_SIZE, dtype=I32).reshape(bp, tp // PAGE_SIZE)
    cmp_p = compress(kv_p.reshape(mp // PAGE_SIZE, PAGE_SIZE, kv_p.shape[1]), own_pages, w_cmp1, w_cmp2, pe_cmp)
    cache_pages = cache_kv.reshape(cache_kv.shape[0], PAGE_SIZE, 2 * kvw)
    cmp_s = compress(cache_pages, page_table, w_cmp1, w_cmp2, pe_cmp)

    hp, hs = norm_pair(xp, xs, norm_g[1, 0], mod[1][0], 3, BF16)
    w_qg = w_b_qg[0]
    w_gate = jnp.pad(w_qg[:, hd:], ((0, 0), (0, LANES - N_HEADS * 3)))
    q_p = q_proj(hp, w_qg, tabs_p, seq_len=tp, tm=ROW_TILE, tn=COL_TILE, per_row=False)
    q_s = q_proj(hs, w_qg, tabs_s, seq_len=ts, tm=ms, tn=COL_TILE, per_row=True)
    gt_p = gate_proj(hp, w_gate, tm=ROW_TILE)
    gt_s = gate_proj(hs, w_gate, tm=ms)
    o_p = nsa_prompt(q_p, gt_p, kvb_p, cmp_p, seq_len=tp, tq=128)

    rows = ts * GROUP
    q_st = q_s.reshape(bs, ts, KV_HEADS, GROUP, HEAD_DIM).transpose(0, 2, 1, 3, 4).reshape(bs, KV_HEADS, rows, HEAD_DIM)
    gt_st = gt_s[:, :N_HEADS * 3].reshape(bs, ts, KV_HEADS, GROUP, 3).transpose(0, 2, 1, 3, 4)
    gt_st = jnp.pad(gt_st.reshape(bs, KV_HEADS, rows, 3), ((0, 0), (0, 0), (0, 0), (0, LANES - 3)))
    kv_s3 = kv_s.reshape(bs, ts, 3 * kvw)
    new_page = jnp.pad(kv_s3[:, :, kvw:2 * kvw], ((0, 0), (0, PAGE_SIZE - ts), (0, 0)))
    win_full = jnp.concatenate([cache_win.reshape(bs, n_buf, kvw), kv_s3[:, :, 2 * kvw:]], axis=1)
    win_pad = jnp.pad(win_full, ((0, 0), (0, WIN_PAD - n_buf - ts), (0, 0)))
    o_st = nsa_sample(q_st, gt_st, cmp_s, cache_pages, page_table, new_page, win_pad, n_tok=ts, n_buf=n_buf)
    o_s = o_st.reshape(bs, KV_HEADS, ts, GROUP, HEAD_DIM).transpose(0, 2, 1, 3, 4).reshape(ms, hd).astype(BF16)

    xp = proj_residual(o_p, w_b_o[0], xp, mod[1][0].prompt(2, 3), rows_per_batch=tp, per_row=False, tm=ROW_TILE,
                       tn=COL_TILE)
    xs = proj_residual(o_s, w_b_o[0], xs, mod[1][0].sample(2, 3), rows_per_batch=ts, per_row=True, tm=ms,
                       tn=COL_TILE)
    xp, xs = _moe_block(xp, xs, norm_g[1, 1][None], mod[1][1], 1, tp, ts, w_router, b_router, w_gu, b_gu, w_down,
                        b_down)

    y_prompt = plain_norm(xp, norm_f[None], ROW_TILE).reshape(bp, tp, d)
    y_sample = plain_norm(xs, norm_f[None], ms).reshape(bs, ts, d)
    kv_p3 = kv_p.reshape(bp, tp, 3 * kvw)
    kv_rows_prompt = kv_p3[:, :, :2 * kvw].reshape(bp, tp, 4, KV_HEADS, HEAD_DIM)
    kv_rows_sample = kv_s3[:, :, :2 * kvw].reshape(bs, ts, 4, KV_HEADS, HEAD_DIM)
    n_win = min(WINDOW, tp)
    win_prompt = kv_p3[:, tp - n_win:, 2 * kvw:].reshape(bp, n_win, 2, KV_HEADS, HEAD_DIM)
    win_sample = win_full[:, ts:].reshape(bs, n_buf, 2, KV_HEADS, HEAD_DIM)
    return (y_prompt, y_sample, kv_rows_prompt, kv_rows_sample, win_prompt, win_sample, conv_p[None], conv_s[None])
```

```python
import functools

import jax
import jax.numpy as jnp
import numpy as np
from jax import lax
from jax.experimental import pallas as pl
from jax.experimental.pallas import tpu as pltpu

F32 = jnp.float32
BF16 = jnp.bfloat16
I32 = jnp.int32

N_HEADS = 16
KV_HEADS = 2
GROUP = N_HEADS // KV_HEADS
HEAD_DIM = 128
ROT_DIM = HEAD_DIM // 4
ROPE_THETA = 500000.0
PAGE_SIZE = 128
CMP_LEN = 32
CMP_STRIDE = 16
SEL_LEN = 64
N_SEL = 16
WINDOW = 512
N_EXPERTS = 32
TOP_K = 4
SWIGLU_LIMIT = 7.0
SWIGLU_ALPHA = 1.702
EPS = 1e-6

V7X_VMEM_BYTES = 64 * 1024 * 1024
VMEM_LIMIT = 56 * 1024 * 1024
NEG = -1e30


def _cparams(sem):
    return pltpu.CompilerParams(dimension_semantics=sem, vmem_limit_bytes=VMEM_LIMIT)


def _bdot(a, b):
    return jnp.dot(a.astype(BF16), b.astype(BF16), preferred_element_type=F32)


def _bdot_t(a, b):
    return lax.dot_general(a.astype(BF16), b.astype(BF16), (((1,), (1,)), ((), ())),
                           preferred_element_type=F32)


def _ada_kernel(c_ref, w_ref, b_ref, o_ref):
    c = c_ref[...]
    a = c * jax.nn.sigmoid(c)
    o_ref[...] = _bdot(a, w_ref[...]) + b_ref[...]


def ada_matmul(c, w, b, tn=1024):
    g, d, n = w.shape
    r = c.shape[0]
    return pl.pallas_call(
        _ada_kernel,
        grid=(g, n // tn),
        in_specs=[pl.BlockSpec((r, d), lambda gi, j: (0, 0)),
                  pl.BlockSpec((None, d, tn), lambda gi, j: (gi, 0, j)),
                  pl.BlockSpec((None, 1, tn), lambda gi, j: (gi, 0, j))],
        out_specs=pl.BlockSpec((None, r, tn), lambda gi, j: (gi, 0, j)),
        out_shape=jax.ShapeDtypeStruct((g, r, n), F32),
        compiler_params=_cparams(("arbitrary", "arbitrary")),
        name="ada_matmul",
    )(c, w, b)


def _norm_body(x, g, sc, sh):
    ms = jnp.mean(x * x, axis=-1, keepdims=True)
    y = x * lax.rsqrt(ms + EPS) * g
    if sc is None:
        return y
    return y * (1.0 + sc) + sh


def _norm_kernel(x_ref, g_ref, sc_ref, sh_ref, o_ref):
    o_ref[...] = _norm_body(x_ref[...], g_ref[...], sc_ref[...], sh_ref[...]).astype(o_ref.dtype)


def _plain_norm_kernel(x_ref, g_ref, o_ref):
    o_ref[...] = _norm_body(x_ref[...], g_ref[...], None, None).astype(o_ref.dtype)


def _mod_specs(per_row, tm, d, rows_per_batch):
    if per_row:
        return pl.BlockSpec((tm, d), lambda i: (i, 0))
    tiles = rows_per_batch // tm
    return pl.BlockSpec((None, 1, d), lambda i: (i // tiles, 0, 0))


def norm_mod(x, g, sc, sh, *, rows_per_batch, per_row, out_dtype, tm):
    m, d = x.shape
    mod = _mod_specs(per_row, tm, d, rows_per_batch)
    return pl.pallas_call(
        _norm_kernel,
        grid=(m // tm,),
        in_specs=[pl.BlockSpec((tm, d), lambda i: (i, 0)),
                  pl.BlockSpec((1, d), lambda i: (0, 0)), mod, mod],
        out_specs=pl.BlockSpec((tm, d), lambda i: (i, 0)),
        out_shape=jax.ShapeDtypeStruct((m, d), out_dtype),
        compiler_params=_cparams(("arbitrary",)),
        name="norm_mod",
    )(x, g, sc, sh)


def plain_norm(x, g, tm):
    m, d = x.shape
    return pl.pallas_call(
        _plain_norm_kernel,
        grid=(m // tm,),
        in_specs=[pl.BlockSpec((tm, d), lambda i: (i, 0)), pl.BlockSpec((1, d), lambda i: (0, 0))],
        out_specs=pl.BlockSpec((tm, d), lambda i: (i, 0)),
        out_shape=jax.ShapeDtypeStruct((m, d), F32),
        compiler_params=_cparams(("arbitrary",)),
        name="plain_norm",
    )(x, g)


def _inproj_common(h_ref, wb_ref, wc_ref, wx_ref, wbf):
    @pl.when(pl.program_id(1) == 0)
    def _cast():
        wbf[0] = wb_ref[...].astype(BF16)
        wbf[1] = wc_ref[...].astype(BF16)
        wbf[2] = wx_ref[...].astype(BF16)

    h = h_ref[...]
    b = jnp.dot(h, wbf[0], preferred_element_type=F32)
    c = jnp.dot(h, wbf[1], preferred_element_type=F32)
    xt = jnp.dot(h, wbf[2], preferred_element_type=F32)
    return b * xt, c


def _inproj_seq_kernel(h_ref, wb_ref, wc_ref, wx_ref, wconv_ref, st_ref, v_ref, so_ref, wbf, carry, *, seq_tiles):
    i = pl.program_id(1)
    u, c = _inproj_common(h_ref, wb_ref, wc_ref, wx_ref, wbf)
    tm = u.shape[0]

    @pl.when(i % seq_tiles == 0)
    def _init():
        carry[0:2, :] = st_ref[...]

    prev0 = carry[0:1, :]
    prev1 = carry[1:2, :]
    rid = lax.broadcasted_iota(I32, u.shape, 0)
    u1 = jnp.where(rid == 0, prev1, pltpu.roll(u, 1, 0))
    u2 = jnp.where(rid == 0, prev0, jnp.where(rid == 1, prev1, pltpu.roll(u, 2, 0)))
    wc = wconv_ref[...]
    conv = wc[0:1, :] * u2 + wc[1:2, :] * u1 + wc[2:3, :] * u
    v_ref[...] = (c * conv).astype(v_ref.dtype)
    last = u[tm - 2:tm, :]
    carry[0:2, :] = last

    @pl.when(i % seq_tiles == seq_tiles - 1)
    def _state():
        so_ref[...] = last


def _inproj_rows_kernel(h_ref, wb_ref, wc_ref, wx_ref, wconv_ref, s1_ref, s2_ref, v_ref, u_ref, wbf, *, seq_len):
    u, c = _inproj_common(h_ref, wb_ref, wc_ref, wx_ref, wbf)
    r = lax.broadcasted_iota(I32, u.shape, 0) % seq_len
    u1 = jnp.where(r >= 1, pltpu.roll(u, 1, 0), s1_ref[...])
    u2 = jnp.where(r >= 2, pltpu.roll(u, 2, 0), s2_ref[...])
    wc = wconv_ref[...]
    conv = wc[0:1, :] * u2 + wc[1:2, :] * u1 + wc[2:3, :] * u
    v_ref[...] = (c * conv).astype(v_ref.dtype)
    u_ref[...] = u


def inproj_seq(h, w_in, w_conv, state, *, seq_len, tm, tn):
    m, k = h.shape
    d = w_in.shape[1] // 3
    nj = d // tn
    seq_tiles = seq_len // tm
    nb = m // seq_len
    wspec = [pl.BlockSpec((k, tn), functools.partial(lambda j, i, o: (0, o + j), o=o * nj)) for o in range(3)]
    return pl.pallas_call(
        functools.partial(_inproj_seq_kernel, seq_tiles=seq_tiles),
        grid=(nj, m // tm),
        in_specs=[pl.BlockSpec((tm, k), lambda j, i: (i, 0))] + wspec + [
            pl.BlockSpec((3, tn), lambda j, i: (0, j)),
            pl.BlockSpec((None, 2, tn), lambda j, i: (i // seq_tiles, 0, j))],
        out_specs=[pl.BlockSpec((tm, tn), lambda j, i: (i, j)),
                   pl.BlockSpec((None, 2, tn), lambda j, i: (i // seq_tiles, 0, j))],
        out_shape=[jax.ShapeDtypeStruct((m, d), BF16), jax.ShapeDtypeStruct((nb, 2, d), F32)],
        scratch_shapes=[pltpu.VMEM((3, k, tn), BF16), pltpu.VMEM((8, tn), F32)],
        compiler_params=_cparams(("arbitrary", "arbitrary")),
        name="inproj_seq",
    )(h, w_in, w_in, w_in, w_conv, state)


def inproj_rows(h, w_in, w_conv, s1, s2, *, seq_len, tn):
    m, k = h.shape
    d = w_in.shape[1] // 3
    nj = d // tn
    wspec = [pl.BlockSpec((k, tn), functools.partial(lambda j, i, o: (0, o + j), o=o * nj)) for o in range(3)]
    row = pl.BlockSpec((m, tn), lambda j, i: (0, j))
    return pl.pallas_call(
        functools.partial(_inproj_rows_kernel, seq_len=seq_len),
        grid=(nj, 1),
        in_specs=[pl.BlockSpec((m, k), lambda j, i: (0, 0))] + wspec + [
            pl.BlockSpec((3, tn), lambda j, i: (0, j)), row, row],
        out_specs=[row, row],
        out_shape=[jax.ShapeDtypeStruct((m, d), BF16), jax.ShapeDtypeStruct((m, d), F32)],
        scratch_shapes=[pltpu.VMEM((3, k, tn), BF16)],
        compiler_params=_cparams(("arbitrary", "arbitrary")),
        name="inproj_rows",
    )(h, w_in, w_in, w_in, w_conv, s1, s2)


def _resid_kernel(a_ref, w_ref, x_ref, g_ref, o_ref, wbf):
    @pl.when(pl.program_id(1) == 0)
    def _cast():
        wbf[...] = w_ref[...].astype(BF16)

    y = jnp.dot(a_ref[...], wbf[...], preferred_element_type=F32)
    o_ref[...] = x_ref[...] + g_ref[...] * y


def proj_residual(a, w, x, gate, *, rows_per_batch, per_row, tm, tn):
    m, k = a.shape
    n = w.shape[1]
    if per_row:
        gspec = pl.BlockSpec((tm, tn), lambda j, i: (i, j))
    else:
        tiles = rows_per_batch // tm
        gspec = pl.BlockSpec((None, 1, tn), lambda j, i: (i // tiles, 0, j))
    return pl.pallas_call(
        _resid_kernel,
        grid=(n // tn, m // tm),
        in_specs=[pl.BlockSpec((tm, k), lambda j, i: (i, 0)),
                  pl.BlockSpec((k, tn), lambda j, i: (0, j)),
                  pl.BlockSpec((tm, tn), lambda j, i: (i, j)), gspec],
        out_specs=pl.BlockSpec((tm, tn), lambda j, i: (i, j)),
        out_shape=jax.ShapeDtypeStruct((m, n), F32),
        scratch_shapes=[pltpu.VMEM((k, tn), BF16)],
        compiler_params=_cparams(("arbitrary", "arbitrary")),
        name="proj_residual",
    )(a, w, x, gate)


LANES = 128
MOE_SUB = 256
MOE_ITEM_SUBS = 5


def _route_kernel(x_ref, g_ref, sc_ref, sh_ref, wr_ref, br_ref, cin_ref,
                  h_ref, ei_ref, wt_ref, rk_ref, cout_ref, cnt):
    @pl.when(pl.program_id(0) == 0)
    def _init():
        cnt[...] = cin_ref[...]

    h = _norm_body(x_ref[...], g_ref[...], sc_ref[...], sh_ref[...])
    nc = h.shape[1] // LANES
    for c in range(nc):
        h_ref[pl.ds(c, h.shape[0], stride=nc), :] = h[:, c * LANES:(c + 1) * LANES]
    logits = jnp.dot(h, wr_ref[...], precision=lax.Precision.HIGHEST, preferred_element_type=F32) + br_ref[...]
    tm, ne = logits.shape
    lane = lax.broadcasted_iota(I32, (tm, ne), 1)
    cur = logits
    vals, idxs = [], []
    for _ in range(TOP_K):
        m = jnp.max(cur, axis=-1, keepdims=True)
        idx = jnp.min(jnp.where(cur == m, lane, ne), axis=-1, keepdims=True)
        vals.append(m)
        idxs.append(idx)
        cur = jnp.where(lane == idx, -jnp.inf, cur)
    es = [jnp.exp(v - vals[0]) for v in vals]
    den = es[0] + es[1] + es[2] + es[3]
    hot = [(lane == idx) for idx in idxs]
    multi = jnp.zeros((tm, ne), F32)
    for hk in hot:
        multi = multi + hk.astype(F32)
    tri = (lax.broadcasted_iota(I32, (tm, tm), 0) > lax.broadcasted_iota(I32, (tm, tm), 1)).astype(BF16)
    base = jnp.dot(tri, multi.astype(BF16), preferred_element_type=F32) + cnt[...]
    wide = lax.broadcasted_iota(I32, (tm, LANES), 1)
    ei = jnp.zeros((tm, LANES), I32)
    wt = jnp.zeros((tm, LANES), F32)
    rk = jnp.zeros((tm, LANES), F32)
    for k in range(TOP_K):
        rank_k = jnp.sum(jnp.where(hot[k], base, 0.0), axis=-1, keepdims=True)
        ei = jnp.where(wide == k, idxs[k], ei)
        wt = jnp.where(wide == k, es[k] / den, wt)
        rk = jnp.where(wide == k, rank_k, rk)
    ei_ref[...] = ei
    wt_ref[...] = wt
    rk_ref[...] = rk
    cnt[...] = cnt[...] + jnp.sum(multi, axis=0, keepdims=True)
    cout_ref[...] = cnt[...]


def route(x, g, sc, sh, w_router, b_router, counts_in, *, rows_per_batch, per_row, tm):
    m, d = x.shape
    ne = w_router.shape[1]
    mod = _mod_specs(per_row, tm, d, rows_per_batch)
    row = pl.BlockSpec((tm, LANES), lambda i: (i, 0))
    one = pl.BlockSpec((1, ne), lambda i: (0, 0))
    return pl.pallas_call(
        _route_kernel,
        grid=(m // tm,),
        in_specs=[pl.BlockSpec((tm, d), lambda i: (i, 0)), pl.BlockSpec((1, d), lambda i: (0, 0)), mod, mod,
                  pl.BlockSpec((d, ne), lambda i: (0, 0)), one, one],
        out_specs=[pl.BlockSpec((tm * (d // LANES), LANES), lambda i: (i, 0)), row, row, row, one],
        out_shape=[jax.ShapeDtypeStruct((m * (d // LANES), LANES), F32), jax.ShapeDtypeStruct((m, LANES), I32),
                   jax.ShapeDtypeStruct((m, LANES), F32), jax.ShapeDtypeStruct((m, LANES), F32),
                   jax.ShapeDtypeStruct((1, ne), F32)],
        scratch_shapes=[pltpu.VMEM((1, ne), F32)],
        compiler_params=_cparams(("arbitrary",)),
        name="route",
    )(x, g, sc, sh, w_router, b_router, counts_in)


def moe_plan(ei, rk, counts, max_items):
    ne = counts.shape[0]
    item_rows = MOE_SUB * MOE_ITEM_SUBS
    padded = (counts + MOE_SUB - 1) // MOE_SUB * MOE_SUB
    pstart = jnp.cumsum(padded) - padded
    pos = pstart[ei] + rk
    zstart = jnp.where(counts % MOE_SUB != 0, pstart + counts // MOE_SUB * MOE_SUB, -1).astype(I32)
    n_chunks = (counts + item_rows - 1) // item_rows
    cum = jnp.cumsum(n_chunks)
    t = jnp.arange(max_items, dtype=I32)
    valid = t < cum[-1]
    t_in = jnp.minimum(t, cum[-1] - 1)
    e = jnp.minimum(jnp.sum((t_in[:, None] >= cum[None, :]).astype(I32), axis=1), ne - 1)
    c = t - (cum[e] - n_chunks[e])
    start = jnp.where(valid, pstart[e] + c * item_rows, 0).astype(I32)
    rows = jnp.where(valid, jnp.minimum(item_rows, counts[e] - c * item_rows), 0).astype(I32)
    return pos.astype(I32), zstart, e, start, rows


def _dispatch_kernel(zs_ref, pos_ref, h_ref, *rest, tm, nc, zero_fill):
    xs_hbm, zbuf, sem, zsem = rest[-4:]
    i = pl.program_id(0)
    if zero_fill:
        @pl.when(i == 0)
        def _zero():
            zbuf[...] = jnp.zeros(zbuf.shape, zbuf.dtype)
            for phase in range(2):
                for e in range(N_EXPERTS):
                    @pl.when(zs_ref[e] >= 0)
                    def _():
                        dst = pl.multiple_of(zs_ref[e] * nc, nc)
                        cp = pltpu.make_async_copy(zbuf, xs_hbm.at[pl.ds(dst, MOE_SUB * nc)], zsem)
                        if phase == 0:
                            cp.start()
                        else:
                            cp.wait()

    def row_copy(t, k):
        src = pl.multiple_of(t * nc, nc)
        dst = pl.multiple_of(pos_ref[0, t * TOP_K + k] * nc, nc)
        return pltpu.make_async_copy(h_ref.at[pl.ds(src, nc)], xs_hbm.at[pl.ds(dst, nc)], sem)

    def start(t, carry):
        for k in range(TOP_K):
            row_copy(t, k).start()
        return carry

    def wait(t, carry):
        for k in range(TOP_K):
            row_copy(t, k).wait()
        return carry

    lax.fori_loop(0, tm, start, 0)
    lax.fori_loop(0, tm, wait, 0)


def dispatch(zstart, pos, h, xs, *, n_rows, tm):
    nc = h.shape[0] // pos.shape[0]
    m = pos.shape[0]
    zero_fill = xs is None
    args = [zstart, pos.reshape(m // tm, 1, tm * TOP_K), h]
    in_specs = [pl.BlockSpec((None, 1, tm * TOP_K), lambda i, zs: (i, 0, 0), memory_space=pltpu.SMEM),
                pl.BlockSpec((tm * nc, LANES), lambda i, zs: (i, 0))]
    aliases = {}
    if not zero_fill:
        args.append(xs)
        in_specs.append(pl.BlockSpec(memory_space=pl.ANY))
        aliases = {3: 0}
    return pl.pallas_call(
        functools.partial(_dispatch_kernel, tm=tm, nc=nc, zero_fill=zero_fill),
        grid_spec=pltpu.PrefetchScalarGridSpec(
            num_scalar_prefetch=1, grid=(m // tm,), in_specs=in_specs,
            out_specs=pl.BlockSpec(memory_space=pl.ANY),
            scratch_shapes=[pltpu.VMEM((MOE_SUB * nc, LANES), F32), pltpu.SemaphoreType.DMA(()),
                            pltpu.SemaphoreType.DMA(())]),
        out_shape=jax.ShapeDtypeStruct((n_rows * nc, LANES), F32),
        input_output_aliases=aliases,
        compiler_params=_cparams(("arbitrary",)),
        name="moe_dispatch",
    )(*args)


def _expert_kernel(ie_ref, is_ref, ir_ref, xs_hbm, wg_ref, wu_ref, wd_ref, bg_ref, bu_ref, bd_ref, ys_hbm,
                   stage, acc, xbf, wgb, wub, wdb, sem_in, sem_out, *, nf):
    del ie_ref
    it = pl.program_id(0)
    f = pl.program_id(1)
    rows = ir_ref[it]
    start = is_ref[it]
    nsub = (rows + MOE_SUB - 1) // MOE_SUB
    nc = stage.shape[1] // MOE_SUB

    def hbm_rows(ref, s):
        first = pl.multiple_of((start + s * MOE_SUB) * nc, MOE_SUB * nc)
        return ref.at[pl.ds(first, MOE_SUB * nc)]

    def copy_in(s):
        return pltpu.make_async_copy(hbm_rows(xs_hbm, s), stage.at[s], sem_in.at[s])

    def copy_out(s):
        return pltpu.make_async_copy(stage.at[s], hbm_rows(ys_hbm, s), sem_out.at[s])

    @pl.when(f == 0)
    def _load():
        for s in range(MOE_ITEM_SUBS):
            @pl.when(s < nsub)
            def _():
                copy_in(s).start()
        for s in range(MOE_ITEM_SUBS):
            @pl.when(s < nsub)
            def _():
                copy_in(s).wait()
                for c in range(nc):
                    xbf[s, :, c * LANES:(c + 1) * LANES] = stage[s, pl.ds(c, MOE_SUB, stride=nc), :].astype(BF16)

    @pl.when(rows > 0)
    def _compute():
        wgb[...] = wg_ref[...].astype(BF16)
        wub[...] = wu_ref[...].astype(BF16)
        wdb[...] = wd_ref[...].astype(BF16)
        for s in range(MOE_ITEM_SUBS):
            @pl.when(s < nsub)
            def _():
                x = xbf[s]
                gate = jnp.dot(x, wgb[...], preferred_element_type=F32) + bg_ref[...]
                up = jnp.dot(x, wub[...], preferred_element_type=F32) + bu_ref[...]
                gate = jnp.minimum(gate, SWIGLU_LIMIT)
                up = jnp.clip(up, -SWIGLU_LIMIT, SWIGLU_LIMIT)
                act = (up + 1.0) * (gate * jax.nn.sigmoid(SWIGLU_ALPHA * gate))
                y = jnp.dot(act.astype(BF16), wdb[...], preferred_element_type=F32)

                @pl.when(f == 0)
                def _():
                    acc[s] = y + bd_ref[...]

                @pl.when(f > 0)
                def _():
                    acc[s] = acc[s] + y

    @pl.when(f == nf - 1)
    def _store():
        for s in range(MOE_ITEM_SUBS):
            @pl.when(s < nsub)
            def _():
                for c in range(nc):
                    stage[s, pl.ds(c, MOE_SUB, stride=nc), :] = acc[s, :, c * LANES:(c + 1) * LANES]
                copy_out(s).start()
        for s in range(MOE_ITEM_SUBS):
            @pl.when(s < nsub)
            def _():
                copy_out(s).wait()


def experts(item_e, item_start, item_rows, xs, w_gu, b_gu, w_down, b_down, layer, *, tf):
    d = w_down.shape[3]
    nc = d // LANES
    d_ff = w_down.shape[2]
    nf = d_ff // tf
    n_items = item_e.shape[0]

    def fblk(it, f, ir):
        return jnp.where(ir[it] > 0, f, nf - 1)

    b_gu4 = b_gu.reshape(b_gu.shape[0], b_gu.shape[1], 1, 2 * d_ff)
    b_down4 = b_down.reshape(b_down.shape[0], b_down.shape[1], 1, d)
    return pl.pallas_call(
        functools.partial(_expert_kernel, nf=nf),
        grid_spec=pltpu.PrefetchScalarGridSpec(
            num_scalar_prefetch=3, grid=(n_items, nf),
            in_specs=[
                pl.BlockSpec(memory_space=pl.ANY),
                pl.BlockSpec((None, None, d, tf), lambda it, f, ie, is_, ir: (layer, ie[it], 0, fblk(it, f, ir))),
                pl.BlockSpec((None, None, d, tf), lambda it, f, ie, is_, ir: (layer, ie[it], 0, nf + fblk(it, f, ir))),
                pl.BlockSpec((None, None, tf, d), lambda it, f, ie, is_, ir: (layer, ie[it], fblk(it, f, ir), 0)),
                pl.BlockSpec((None, None, 1, tf), lambda it, f, ie, is_, ir: (layer, ie[it], 0, fblk(it, f, ir))),
                pl.BlockSpec((None, None, 1, tf), lambda it, f, ie, is_, ir: (layer, ie[it], 0, nf + fblk(it, f, ir))),
                pl.BlockSpec((None, None, 1, d), lambda it, f, ie, is_, ir: (layer, ie[it], 0, 0)),
            ],
            out_specs=pl.BlockSpec(memory_space=pl.ANY),
            scratch_shapes=[pltpu.VMEM((MOE_ITEM_SUBS, MOE_SUB * nc, LANES), F32),
                            pltpu.VMEM((MOE_ITEM_SUBS, MOE_SUB, d), F32),
                            pltpu.VMEM((MOE_ITEM_SUBS, MOE_SUB, d), BF16),
                            pltpu.VMEM((d, tf), BF16), pltpu.VMEM((d, tf), BF16), pltpu.VMEM((tf, d), BF16),
                            pltpu.SemaphoreType.DMA((MOE_ITEM_SUBS,)), pltpu.SemaphoreType.DMA((MOE_ITEM_SUBS,))]),
        out_shape=jax.ShapeDtypeStruct(xs.shape, F32),
        input_output_aliases={3: 0},
        compiler_params=_cparams(("arbitrary", "arbitrary")),
        name="moe_experts",
    )(item_e, item_start, item_rows, xs, w_gu, w_gu, w_down, b_gu4, b_gu4, b_down4)


def _combine_kernel(pos_ref, ys_hbm, wt_ref, x_ref, g_ref, o_ref, buf, sem, *, tm):
    nc = buf.shape[1] // tm

    def row_copy(t, k):
        src = pl.multiple_of(pos_ref[0, t * TOP_K + k] * nc, nc)
        dst = pl.multiple_of(t * nc, nc)
        return pltpu.make_async_copy(ys_hbm.at[pl.ds(src, nc)], buf.at[k, pl.ds(dst, nc)], sem)

    def start(t, carry):
        for k in range(TOP_K):
            row_copy(t, k).start()
        return carry

    def wait(t, carry):
        for k in range(TOP_K):
            row_copy(t, k).wait()
        return carry

    lax.fori_loop(0, tm, start, 0)
    lax.fori_loop(0, tm, wait, 0)
    wt = wt_ref[...]
    for c in range(nc):
        y = wt[:, 0:1] * buf[0, pl.ds(c, tm, stride=nc), :]
        for k in range(1, TOP_K):
            y = y + wt[:, k:k + 1] * buf[k, pl.ds(c, tm, stride=nc), :]
        cols = slice(c * LANES, (c + 1) * LANES)
        o_ref[:, cols] = x_ref[:, cols] + g_ref[:, cols] * y


def combine(pos, ys, wt, x, gate, *, rows_per_batch, per_row, tm):
    m, d = x.shape
    nc = d // LANES
    if per_row:
        gspec = pl.BlockSpec((tm, d), lambda i: (i, 0))
    else:
        tiles = rows_per_batch // tm
        gspec = pl.BlockSpec((None, 1, d), lambda i: (i // tiles, 0, 0))
    return pl.pallas_call(
        functools.partial(_combine_kernel, tm=tm),
        grid=(m // tm,),
        in_specs=[pl.BlockSpec((None, 1, tm * TOP_K), lambda i: (i, 0, 0), memory_space=pltpu.SMEM),
                  pl.BlockSpec(memory_space=pl.ANY),
                  pl.BlockSpec((tm, LANES), lambda i: (i, 0)),
                  pl.BlockSpec((tm, d), lambda i: (i, 0)), gspec],
        out_specs=pl.BlockSpec((tm, d), lambda i: (i, 0)),
        out_shape=jax.ShapeDtypeStruct((m, d), F32),
        scratch_shapes=[pltpu.VMEM((TOP_K, tm * nc, LANES), F32), pltpu.SemaphoreType.DMA(())],
        compiler_params=_cparams(("arbitrary",)),
        name="moe_combine",
    )(pos.reshape(m // tm, 1, tm * TOP_K), ys, wt, x, gate)


def rotary_tables(pos):
    half = ROT_DIM // 2
    inv = ROPE_THETA ** (-jnp.arange(0, ROT_DIM, 2, dtype=F32) / ROT_DIM)
    ang = pos.astype(F32)[:, None] * inv[None, :]
    cos, sin = jnp.cos(ang), jnp.sin(ang)
    n = pos.shape[0]
    rest = HEAD_DIM - ROT_DIM
    c = jnp.concatenate([cos, cos, jnp.ones((n, rest), F32)], axis=1)
    a = jnp.concatenate([-sin, jnp.zeros((n, HEAD_DIM - half), F32)], axis=1)
    b = jnp.concatenate([jnp.zeros((n, half), F32), sin, jnp.zeros((n, rest), F32)], axis=1)
    return c, a, b


def _rot(seg, c, a, b):
    half = ROT_DIM // 2
    return seg * c + pltpu.roll(seg, HEAD_DIM - half, 1) * a + pltpu.roll(seg, half, 1) * b


def _kv_kernel(h_ref, w_ref, c_ref, a_ref, b_ref, o_ref, *rest, head_major):
    if head_major:
        ob_ref, wbf = rest
    else:
        (wbf,) = rest

    @pl.when(pl.program_id(1) == 0)
    def _cast():
        wbf[...] = w_ref[...].astype(BF16)

    y = jnp.dot(h_ref[...], wbf[...], preferred_element_type=F32)
    c, a, b = c_ref[...], a_ref[...], b_ref[...]
    for s in range(2 * KV_HEADS):
        seg = y[:, s * HEAD_DIM:(s + 1) * HEAD_DIM]
        if s < KV_HEADS:
            seg = _rot(seg, c, a, b)
        o_ref[:, s * HEAD_DIM:(s + 1) * HEAD_DIM] = seg
        if head_major:
            ob_ref[s] = seg.astype(BF16)


def kv_proj(h, w_kv, tabs, *, seq_len, tm, head_major):
    m, k = h.shape
    n = w_kv.shape[1]
    tn = 2 * KV_HEADS * HEAD_DIM
    seq_tiles = seq_len // tm if head_major else 1
    tab = pl.BlockSpec((tm, HEAD_DIM), lambda j, i: (i % seq_tiles, 0))
    out_specs = [pl.BlockSpec((tm, tn), lambda j, i: (i, j))]
    out_shape = [jax.ShapeDtypeStruct((m, n), F32)]
    if head_major:
        out_specs.append(pl.BlockSpec((None, 2 * KV_HEADS, tm, HEAD_DIM),
                                      lambda j, i: (i // seq_tiles, j, i % seq_tiles, 0)))
        out_shape.append(jax.ShapeDtypeStruct((m // seq_len, n // HEAD_DIM, seq_len, HEAD_DIM), BF16))
    return pl.pallas_call(
        functools.partial(_kv_kernel, head_major=head_major),
        grid=(n // tn, m // tm),
        in_specs=[pl.BlockSpec((tm, k), lambda j, i: (i, 0)), pl.BlockSpec((k, tn), lambda j, i: (0, j)),
                  tab, tab, tab],
        out_specs=out_specs, out_shape=out_shape,
        scratch_shapes=[pltpu.VMEM((k, tn), BF16)],
        compiler_params=_cparams(("arbitrary", "arbitrary")),
        name="kv_proj",
    )(h, w_kv, *tabs)


def _q_kernel(h_ref, w_ref, c_ref, a_ref, b_ref, o_ref, wbf, *, scale):
    @pl.when(pl.program_id(1) == 0)
    def _cast():
        wbf[...] = w_ref[...].astype(BF16)

    y = jnp.dot(h_ref[...], wbf[...], preferred_element_type=F32)
    c, a, b = c_ref[...], a_ref[...], b_ref[...]
    for s in range(y.shape[1] // HEAD_DIM):
        seg = _rot(y[:, s * HEAD_DIM:(s + 1) * HEAD_DIM], c, a, b)
        o_ref[:, s * HEAD_DIM:(s + 1) * HEAD_DIM] = (seg * scale).astype(o_ref.dtype)


def q_proj(h, w_qg, tabs, *, seq_len, tm, tn, per_row):
    m, k = h.shape
    n = N_HEADS * HEAD_DIM
    seq_tiles = 1 if per_row else seq_len // tm
    tab = pl.BlockSpec((tm, HEAD_DIM), lambda j, i: (i % seq_tiles, 0))
    return pl.pallas_call(
        functools.partial(_q_kernel, scale=HEAD_DIM ** -0.5),
        grid=(n // tn, m // tm),
        in_specs=[pl.BlockSpec((tm, k), lambda j, i: (i, 0)), pl.BlockSpec((k, tn), lambda j, i: (0, j)),
                  tab, tab, tab],
        out_specs=pl.BlockSpec((tm, tn), lambda j, i: (i, j)),
        out_shape=jax.ShapeDtypeStruct((m, n), BF16),
        scratch_shapes=[pltpu.VMEM((k, tn), BF16)],
        compiler_params=_cparams(("arbitrary", "arbitrary")),
        name="q_proj",
    )(h, w_qg, *tabs)


def _gate_kernel(h_ref, w_ref, o_ref):
    o_ref[...] = jax.nn.sigmoid(_bdot(h_ref[...], w_ref[...]))


def gate_proj(h, w_g, *, tm):
    m, k = h.shape
    return pl.pallas_call(
        _gate_kernel,
        grid=(m // tm,),
        in_specs=[pl.BlockSpec((tm, k), lambda i: (i, 0)), pl.BlockSpec((k, LANES), lambda i: (0, 0))],
        out_specs=pl.BlockSpec((tm, LANES), lambda i: (i, 0)),
        out_shape=jax.ShapeDtypeStruct((m, LANES), F32),
        compiler_params=_cparams(("arbitrary",)),
        name="gate_proj",
    )(h, w_g)


CMP_PAGES = 16
CHUNKS_PER_PAGE = PAGE_SIZE // CMP_STRIDE


def _compress_kernel(pt_ref, *refs, n_steps):
    del pt_ref
    pages = refs[:CMP_PAGES]
    w1_ref, pe_ref, w2_ref, o_ref, p_sc, head_sc = refs[CMP_PAGES:]
    j = pl.program_id(1)
    rows = CMP_PAGES * CHUNKS_PER_PAGE
    flat = CMP_STRIDE * HEAD_DIM
    n_slots = 2 * KV_HEADS
    for p, pg in enumerate(pages):
        for c in range(n_slots):
            head_sc[p * n_slots + c] = pg[:, c * HEAD_DIM:(c + 1) * HEAD_DIM]
    for kv in range(2):
        xs = []
        for g in range(KV_HEADS):
            per_page = []
            for p in range(CMP_PAGES):
                slot = p * n_slots + kv * KV_HEADS + g
                pieces = [head_sc[slot, pl.ds(s, CHUNKS_PER_PAGE, stride=CMP_STRIDE), :] for s in range(CMP_STRIDE)]
                per_page.append(jnp.concatenate(pieces, axis=1))
            xs.append(jnp.concatenate(per_page, axis=0))
        x = jnp.concatenate(xs, axis=0)
        for r in range(CMP_LEN // CMP_STRIDE):
            xr = x + pe_ref[kv, :, r * flat:(r + 1) * flat]
            pr = _bdot(xr, w1_ref[kv, r * flat:(r + 1) * flat, :])
            for g in range(KV_HEADS):
                p_sc[r, kv * KV_HEADS + g, pl.ds(j * rows, rows), :] = pr[g * rows:(g + 1) * rows]

    @pl.when(j == n_steps - 1)
    def _finish():
        n = p_sc.shape[2]
        for kv in range(2):
            for g in range(KV_HEADS):
                c = kv * KV_HEADS + g
                hid = p_sc[0, c] + pltpu.roll(p_sc[1, c], n - 1, 0)
                act = hid * jax.nn.sigmoid(hid)
                o_ref[c] = _bdot(act, w2_ref[kv]).astype(o_ref.dtype)


def compress(pages, page_table, w_cmp1, w_cmp2, pe_cmp):
    nb, n_pages = page_table.shape
    n_steps = n_pages // CMP_PAGES
    n_chunks = n_pages * CHUNKS_PER_PAGE
    width = 2 * KV_HEADS * HEAD_DIM
    flat = CMP_LEN * HEAD_DIM
    page_specs = [pl.BlockSpec((None, PAGE_SIZE, width),
                               functools.partial(lambda b, j, pt, p: (pt[b, j * CMP_PAGES + p], 0, 0), p=p))
                  for p in range(CMP_PAGES)]
    return pl.pallas_call(
        functools.partial(_compress_kernel, n_steps=n_steps),
        grid_spec=pltpu.PrefetchScalarGridSpec(
            num_scalar_prefetch=1, grid=(nb, n_steps),
            in_specs=page_specs + [
                pl.BlockSpec((2, flat, HEAD_DIM), lambda b, j, pt: (0, 0, 0)),
                pl.BlockSpec((2, 1, flat), lambda b, j, pt: (0, 0, 0)),
                pl.BlockSpec((2, HEAD_DIM, HEAD_DIM), lambda b, j, pt: (0, 0, 0))],
            out_specs=pl.BlockSpec((None, 2 * KV_HEADS, n_chunks, HEAD_DIM), lambda b, j, pt: (b, 0, 0, 0)),
            scratch_shapes=[pltpu.VMEM((2, 2 * KV_HEADS, n_chunks, HEAD_DIM), F32),
                            pltpu.VMEM((CMP_PAGES * 2 * KV_HEADS, PAGE_SIZE, HEAD_DIM), F32)]),
        out_shape=jax.ShapeDtypeStruct((nb, 2 * KV_HEADS, n_chunks, HEAD_DIM), BF16),
        compiler_params=_cparams(("arbitrary", "arbitrary")),
        name="compress",
    )(page_table, *([pages] * CMP_PAGES), w_cmp1.reshape(2, flat, HEAD_DIM), pe_cmp.reshape(2, 1, flat), w_cmp2)


def _softmax_rows(s, mask):
    sm = jnp.where(mask, s, NEG)
    m = jnp.max(sm, axis=-1, keepdims=True)
    p = jnp.where(mask, jnp.exp(sm - m), 0.0)
    den = jnp.sum(p, axis=-1, keepdims=True)
    return p / jnp.where(den > 0, den, 1.0)


def _overlap(n_cmp_blocks, n_sel_blocks, transposed=False):
    shape = (n_sel_blocks, n_cmp_blocks) if transposed else (n_cmp_blocks, n_sel_blocks)
    ci = lax.broadcasted_iota(I32, shape, 1 if transposed else 0) * CMP_STRIDE
    sj = lax.broadcasted_iota(I32, shape, 0 if transposed else 1) * SEL_LEN
    return ((ci <= sj + SEL_LEN - 1) & (ci + CMP_LEN - 1 >= sj)).astype(F32)


def _lane_tile(x, n):
    return x if n == LANES else jnp.concatenate([x] * (n // LANES), axis=1)


def _online_update(s, v, m_sc, l_sc, acc_sc):
    m_old = m_sc[...]
    m_new = jnp.maximum(m_old, jnp.max(s, axis=-1, keepdims=True))
    alpha = jnp.exp(m_old - m_new)
    p = jnp.exp(s - _lane_tile(m_new, s.shape[1]))
    l_sc[...] = alpha * l_sc[...] + jnp.sum(p, axis=-1, keepdims=True)
    acc_sc[...] = alpha * acc_sc[...] + _bdot(p, v)
    m_sc[...] = m_new


def _reset(m_sc, l_sc, acc_sc):
    m_sc[...] = jnp.full(m_sc.shape, NEG, F32)
    l_sc[...] = jnp.zeros(l_sc.shape, F32)
    acc_sc[...] = jnp.zeros(acc_sc.shape, F32)


ATT_CK = 256
ATT_RB = 64


def _flash_chunk(q_sc, k, v, bias, s_sc, p_sc, al_sc, m_sc, l_sc, acc_sc, *, tq):
    rows, ck = s_sc.shape
    s_sc[...] = _bdot_t(q_sc[...], k)
    for rb in range(rows // ATT_RB):
        r = slice(rb * ATT_RB, (rb + 1) * ATT_RB)
        q0 = (rb * ATT_RB) % tq
        s = s_sc[r, :] + bias[q0:q0 + ATT_RB]
        m_old = m_sc[r, :]
        m_new = jnp.maximum(m_old, jnp.max(s, axis=-1, keepdims=True))
        alpha = jnp.exp(m_old - m_new)
        p = jnp.exp(s - _lane_tile(m_new, ck))
        l_sc[r, :] = alpha * l_sc[r, :] + jnp.sum(p, axis=-1, keepdims=True)
        m_sc[r, :] = m_new
        al_sc[r, :] = alpha
        p_sc[r, :] = p.astype(BF16)
    acc_sc[...] = al_sc[...] * acc_sc[...] + jnp.dot(p_sc[...], v, preferred_element_type=F32)


def _nsa_prompt_kernel(q_ref, gt_ref, slc_ref, win_ref, cmp_ref, o_ref, selx, q_sc, s_sc, p_sc, al_sc, m_sc, l_sc,
                       acc_sc, *, tq, n_cmp, seq_len):
    i = pl.program_id(1)
    s0 = i * tq
    rows = GROUP * tq
    n_chunks = cmp_ref.shape[1]
    nb = seq_len // SEL_LEN
    tpos = s0 + lax.broadcasted_iota(I32, (tq, 1), 0)
    tpos_h = s0 + lax.broadcasted_iota(I32, (rows, 1), 0) % tq
    gates = gt_ref[...]
    for g in range(KV_HEADS):
        qg = jnp.concatenate([q_ref[:, (g * GROUP + h) * HEAD_DIM:(g * GROUP + h + 1) * HEAD_DIM]
                              for h in range(GROUP)], axis=0)
        n_id = lax.broadcasted_iota(I32, (1, n_chunks), 1)
        mask_c = (n_id * CMP_STRIDE + CMP_LEN - 1 <= tpos_h) & (n_id < n_cmp)
        p_c = _softmax_rows(_bdot_t(qg, cmp_ref[g]), mask_c)
        o_c = _bdot(p_c, cmp_ref[KV_HEADS + g])
        psum = p_c[0:tq]
        for h in range(1, GROUP):
            psum = psum + p_c[h * tq:(h + 1) * tq]
        imp = jnp.dot(psum, _overlap(n_chunks, nb), precision=lax.Precision.HIGHEST, preferred_element_type=F32)
        jj = lax.broadcasted_iota(I32, (tq, nb), 1)
        bt = tpos // SEL_LEN
        forced = (jj == 0) | (jj == bt) | (jj == bt - 1)
        key = jnp.where(forced, jnp.inf, jnp.where(jj <= bt, imp, -jnp.inf))
        rank = jnp.zeros((tq, nb), F32)
        for c in range(nb):
            col = key[:, c:c + 1]
            ahead = (col > key) | ((col == key) & (jj > c))
            rank = rank + ahead.astype(F32)
        sel = ((rank < N_SEL) & (jj <= bt)).astype(BF16)
        expand = (lax.broadcasted_iota(I32, (nb, seq_len), 1) // SEL_LEN
                  == lax.broadcasted_iota(I32, (nb, seq_len), 0)).astype(BF16)
        selk = jnp.dot(sel, expand, preferred_element_type=F32)
        for c in range(seq_len // ATT_CK):
            selx[c] = (selk[:, c * ATT_CK:(c + 1) * ATT_CK] - 1.0) * (-NEG)
        q_sc[...] = qg
        last = (s0 + tq + ATT_CK - 1) // ATT_CK

        _reset(m_sc, l_sc, acc_sc)

        def sel_step(c, carry):
            off = pl.multiple_of(c * ATT_CK, ATT_CK)
            kpos = off + lax.broadcasted_iota(I32, (1, ATT_CK), 1)
            bias = selx[c] + jnp.where(kpos <= tpos, 0.0, NEG)
            _flash_chunk(q_sc, slc_ref[g, pl.ds(off, ATT_CK), :], slc_ref[KV_HEADS + g, pl.ds(off, ATT_CK), :],
                         bias, s_sc, p_sc, al_sc, m_sc, l_sc, acc_sc, tq=tq)
            return carry

        lax.fori_loop(0, last, sel_step, 0)
        o_s = acc_sc[...] / l_sc[...]

        _reset(m_sc, l_sc, acc_sc)

        def win_step(c, carry):
            off = pl.multiple_of(c * ATT_CK, ATT_CK)
            kpos = off + lax.broadcasted_iota(I32, (1, ATT_CK), 1)
            bias = jnp.where(kpos <= tpos, 0.0, NEG) + jnp.where(tpos - kpos < WINDOW, 0.0, NEG)
            _flash_chunk(q_sc, win_ref[g, pl.ds(off, ATT_CK), :], win_ref[KV_HEADS + g, pl.ds(off, ATT_CK), :],
                         bias, s_sc, p_sc, al_sc, m_sc, l_sc, acc_sc, tq=tq)
            return carry

        lax.fori_loop(jnp.maximum(s0 - WINDOW + 1, 0) // ATT_CK, last, win_step, 0)
        o_w = acc_sc[...] / l_sc[...]

        for h in range(GROUP):
            hh = g * GROUP + h
            r0 = h * tq
            o_h = (gates[:, 3 * hh:3 * hh + 1] * o_c[r0:r0 + tq]
                   + gates[:, 3 * hh + 1:3 * hh + 2] * o_s[r0:r0 + tq]
                   + gates[:, 3 * hh + 2:3 * hh + 3] * o_w[r0:r0 + tq])
            o_ref[:, hh * HEAD_DIM:(hh + 1) * HEAD_DIM] = o_h.astype(o_ref.dtype)


def nsa_prompt(q, gates, kvb, cmp, *, seq_len, tq):
    m, hd = q.shape
    nbatch = m // seq_len
    tiles = seq_len // tq
    n_chunks = cmp.shape[2]
    n_cmp = seq_len // CMP_STRIDE - CMP_LEN // CMP_STRIDE + 1
    rows = GROUP * tq
    blk = 2 * KV_HEADS
    return pl.pallas_call(
        functools.partial(_nsa_prompt_kernel, tq=tq, n_cmp=n_cmp, seq_len=seq_len),
        grid=(nbatch, tiles),
        in_specs=[pl.BlockSpec((tq, hd), lambda b, i: (b * tiles + i, 0)),
                  pl.BlockSpec((tq, LANES), lambda b, i: (b * tiles + i, 0)),
                  pl.BlockSpec((None, blk, seq_len, HEAD_DIM), lambda b, i: (b, 1, 0, 0)),
                  pl.BlockSpec((None, blk, seq_len, HEAD_DIM), lambda b, i: (b, 2, 0, 0)),
                  pl.BlockSpec((None, blk, n_chunks, HEAD_DIM), lambda b, i: (b, 0, 0, 0))],
        out_specs=pl.BlockSpec((tq, hd), lambda b, i: (b * tiles + i, 0)),
        out_shape=jax.ShapeDtypeStruct((m, hd), BF16),
        scratch_shapes=[pltpu.VMEM((seq_len // ATT_CK, tq, ATT_CK), F32),
                        pltpu.VMEM((rows, HEAD_DIM), BF16), pltpu.VMEM((rows, ATT_CK), F32),
                        pltpu.VMEM((rows, ATT_CK), BF16), pltpu.VMEM((rows, LANES), F32),
                        pltpu.VMEM((rows, LANES), F32), pltpu.VMEM((rows, LANES), F32),
                        pltpu.VMEM((rows, HEAD_DIM), F32)],
        compiler_params=_cparams(("arbitrary", "arbitrary")),
        name="nsa_prompt",
    )(q, gates, kvb, kvb, cmp)


SMP_PAGES = 8
SEL_PAD = 384
WIN_PAD = 640


def _nsa_sample_kernel(pt_ref, q_ref, gt_ref, cmp_ref, *refs, past_len, n_tok, n_buf, n_cmp):
    del pt_ref
    pages = refs[:SMP_PAGES]
    new_ref, win_ref, o_ref, selx, oc_sc, m_sc, l_sc, acc_sc = refs[SMP_PAGES:]
    j = pl.program_id(1)
    n_steps = pl.num_programs(1)
    rows = n_tok * GROUP
    n_chunks = cmp_ref.shape[1]
    n_pages = past_len // PAGE_SIZE
    nb = -(-(past_len + n_tok) // SEL_LEN)
    tpos_r = past_len + lax.broadcasted_iota(I32, (rows, 1), 0) // GROUP
    tpos_q = past_len + lax.broadcasted_iota(I32, (n_tok, 1), 0)

    @pl.when(j == 0)
    def _select():
        sels = []
        for g in range(KV_HEADS):
            qg = q_ref[g]
            n_id = lax.broadcasted_iota(I32, (1, n_chunks), 1)
            mask_c = (n_id * CMP_STRIDE + CMP_LEN - 1 <= tpos_r) & (n_id < n_cmp)
            p_c = _softmax_rows(_bdot_t(qg, cmp_ref[g]), mask_c)
            oc_sc[g] = _bdot(p_c, cmp_ref[KV_HEADS + g])
            psum = jnp.sum(p_c.reshape(n_tok, GROUP, n_chunks), axis=1)
            imp = jnp.dot(psum, _overlap(n_chunks, SEL_PAD), precision=lax.Precision.HIGHEST,
                          preferred_element_type=F32)
            jj = lax.broadcasted_iota(I32, (n_tok, SEL_PAD), 1)
            bt = tpos_q // SEL_LEN
            forced = (jj == 0) | (jj == bt) | (jj == bt - 1)
            key = jnp.where(jj < nb, jnp.where(forced, jnp.inf, jnp.where(jj <= bt, imp, -jnp.inf)), -jnp.inf)
            alive = jj >= 0
            for _ in range(N_SEL):
                cur = jnp.where(alive, key, -jnp.inf)
                mx = jnp.max(cur, axis=-1, keepdims=True)
                idx = jnp.min(jnp.where(alive & (cur == mx), jj, SEL_PAD), axis=-1, keepdims=True)
                alive = alive & (jj != idx)
            sels.append(((~alive) & (jj <= bt) & (jj < nb)).astype(F32))
        sel = jnp.concatenate(sels + [jnp.zeros((8 - KV_HEADS * n_tok, SEL_PAD), F32)] * (KV_HEADS * n_tok < 8),
                              axis=0)

        def fill(pg, carry):
            blk = lax.broadcasted_iota(I32, (SEL_PAD, PAGE_SIZE), 0)
            kk = lax.broadcasted_iota(I32, (SEL_PAD, PAGE_SIZE), 1)
            expand = (blk == pg * (PAGE_SIZE // SEL_LEN) + kk // SEL_LEN).astype(F32)
            e = jnp.dot(sel, expand, preferred_element_type=F32)
            for g in range(KV_HEADS):
                rows_g = jnp.concatenate(
                    [jnp.broadcast_to(e[g * n_tok + q:g * n_tok + q + 1], (GROUP, PAGE_SIZE)) for q in range(n_tok)],
                    axis=0)
                selx[pg, g] = (rows_g - 1.0) * (-NEG)
            return carry

        lax.fori_loop(0, n_pages + 1, fill, 0)
        _reset(m_sc, l_sc, acc_sc)

    def head_cols(ref, slot, g):
        c0 = (slot * KV_HEADS + g) * HEAD_DIM
        return ref[:, c0:c0 + HEAD_DIM]

    @pl.when(j < n_steps - 1)
    def _pages():
        for g in range(KV_HEADS):
            k = jnp.concatenate([head_cols(pg, 0, g).astype(BF16) for pg in pages], axis=0)
            v = jnp.concatenate([head_cols(pg, 1, g).astype(BF16) for pg in pages], axis=0)
            bias = jnp.concatenate([selx[j * SMP_PAGES + p, g] for p in range(SMP_PAGES)], axis=1)
            _online_update(_bdot_t(q_ref[g], k) + bias, v, m_sc.at[g], l_sc.at[g], acc_sc.at[g])

    @pl.when(j == n_steps - 1)
    def _finish():
        kpos = past_len + lax.broadcasted_iota(I32, (1, PAGE_SIZE), 1)
        for g in range(KV_HEADS):
            bias = selx[n_pages, g] + jnp.where(kpos <= tpos_r, 0.0, NEG)
            _online_update(_bdot_t(q_ref[g], head_cols(new_ref, 0, g)) + bias, head_cols(new_ref, 1, g),
                           m_sc.at[g], l_sc.at[g], acc_sc.at[g])
        idx = lax.broadcasted_iota(I32, (1, WIN_PAD), 1)
        dt = tpos_r - (past_len - n_buf + idx)
        allow_w = (dt >= 0) & (dt < WINDOW) & (idx < n_buf + n_tok)
        for g in range(KV_HEADS):
            kw = win_ref[:, g * HEAD_DIM:(g + 1) * HEAD_DIM]
            vw = win_ref[:, (KV_HEADS + g) * HEAD_DIM:(KV_HEADS + g + 1) * HEAD_DIM]
            p_w = _softmax_rows(_bdot_t(q_ref[g], kw), allow_w)
            o_w = _bdot(p_w, vw)
            o_s = acc_sc[g] / l_sc[g]
            gt = gt_ref[g]
            o_ref[g] = gt[:, 0:1] * oc_sc[g] + gt[:, 1:2] * o_s + gt[:, 2:3] * o_w


def nsa_sample(q, gates, cmp, pages, page_table, new_page, win_full, *, n_tok, n_buf):
    nb, n_pages = page_table.shape
    past_len = n_pages * PAGE_SIZE
    rows = n_tok * GROUP
    n_chunks = cmp.shape[2]
    n_cmp = (past_len + n_tok) // CMP_STRIDE - CMP_LEN // CMP_STRIDE + 1
    width = 2 * KV_HEADS * HEAD_DIM
    n_steps = n_pages // SMP_PAGES + 1
    blk = pl.BlockSpec((None, KV_HEADS, rows, HEAD_DIM), lambda b, j, pt: (b, 0, 0, 0))
    page_specs = [pl.BlockSpec(
        (None, PAGE_SIZE, width),
        functools.partial(lambda b, j, pt, p: (pt[b, jnp.minimum(j * SMP_PAGES + p, n_pages - 1)], 0, 1), p=p))
        for p in range(SMP_PAGES)]
    return pl.pallas_call(
        functools.partial(_nsa_sample_kernel, past_len=past_len, n_tok=n_tok, n_buf=n_buf, n_cmp=n_cmp),
        grid_spec=pltpu.PrefetchScalarGridSpec(
            num_scalar_prefetch=1, grid=(nb, n_steps),
            in_specs=[blk, blk, pl.BlockSpec((None, 2 * KV_HEADS, n_chunks, HEAD_DIM), lambda b, j, pt: (b, 0, 0, 0))]
            + page_specs + [pl.BlockSpec((None, PAGE_SIZE, width), lambda b, j, pt: (b, 0, 0)),
                            pl.BlockSpec((None, WIN_PAD, width), lambda b, j, pt: (b, 0, 0))],
            out_specs=blk,
            scratch_shapes=[pltpu.VMEM((n_pages + 1, KV_HEADS, rows, PAGE_SIZE), F32),
                            pltpu.VMEM((KV_HEADS, rows, HEAD_DIM), F32),
                            pltpu.VMEM((KV_HEADS, rows, LANES), F32), pltpu.VMEM((KV_HEADS, rows, LANES), F32),
                            pltpu.VMEM((KV_HEADS, rows, HEAD_DIM), F32)]),
        out_shape=jax.ShapeDtypeStruct((nb, KV_HEADS, rows, HEAD_DIM), F32),
        compiler_params=_cparams(("arbitrary", "arbitrary")),
        name="nsa_sample",
    )(page_table, q, gates, cmp, *([pages] * SMP_PAGES), new_page, win_full)


ROW_TILE = 512
COL_TILE = 512
ADA_ROWS = 16


class _Group:
    def __init__(self, vec, n_prompt, n_sample, n_tok):
        self.vec, self.n_prompt, self.n_sample, self.n_tok = vec, n_prompt, n_sample, n_tok

    def prompt(self, k, n):
        d = self.vec.shape[-1] // n
        return self.vec[:self.n_prompt, None, k * d:(k + 1) * d]

    def sample(self, k, n):
        d = self.vec.shape[-1] // n
        v = self.vec[self.n_prompt:self.n_prompt + self.n_sample, k * d:(k + 1) * d]
        return jnp.repeat(v, self.n_tok, axis=0)


def _moe_block(xp, xs, g_norm, mod, layer, tp, ts, w_router, b_router, w_gu, b_gu, w_down, b_down):
    mp, d = xp.shape
    ms = xs.shape[0]
    ne = w_router.shape[-1]
    wr, br = w_router[layer], b_router[layer][None]
    cin = jnp.zeros((1, ne), F32)
    hp, eip, wtp, rkp, c1 = route(xp, g_norm, mod.prompt(1, 3), mod.prompt(0, 3), wr, br, cin,
                                  rows_per_batch=tp, per_row=False, tm=256)
    hs, eis, wts, rks, c2 = route(xs, g_norm, mod.sample(1, 3), mod.sample(0, 3), wr, br, c1,
                                  rows_per_batch=ts, per_row=True, tm=ms)
    counts = c2[0].astype(I32)
    ei = jnp.concatenate([eip, eis])[:, :TOP_K]
    rk = jnp.concatenate([rkp, rks])[:, :TOP_K].astype(I32)
    nk = (mp + ms) * TOP_K
    n_rows = nk + ne * MOE_SUB
    max_items = ne + -(-nk // (MOE_SUB * MOE_ITEM_SUBS))
    pos, zstart, item_e, item_start, item_rows = moe_plan(ei, rk, counts, max_items)
    buf = dispatch(zstart, pos[:mp], hp, None, n_rows=n_rows, tm=256)
    buf = dispatch(zstart, pos[mp:], hs, buf, n_rows=n_rows, tm=ms)
    buf = experts(item_e, item_start, item_rows, buf, w_gu, b_gu, w_down, b_down, layer, tf=256)
    xp = combine(pos[:mp], buf, wtp, xp, mod.prompt(2, 3), rows_per_batch=tp, per_row=False, tm=128)
    xs = combine(pos[mp:], buf, wts, xs, mod.sample(2, 3), rows_per_batch=ts, per_row=True, tm=ms)
    return xp, xs


def kernel(x_prompt, x_sample, cache_kv, cache_win, state_conv, page_table, c_prompt, c_sample, norm_g, w_ada, b_ada,
           w_a_in, w_a_conv, w_a_out, norm_kv, w_ada_kv, b_ada_kv, w_kv, w_cmp1, w_cmp2, pe_cmp, w_b_qg, w_b_o,
           w_router, b_router, w_gu, b_gu, w_down, b_down, norm_f):
    bp, tp, d = x_prompt.shape
    bs, ts, _ = x_sample.shape
    mp, ms = bp * tp, bs * ts
    depth = w_ada.shape[0]
    n_pages = page_table.shape[1]
    past_len = n_pages * PAGE_SIZE
    n_buf = cache_win.shape[1]
    hd = N_HEADS * HEAD_DIM
    kvw = 2 * KV_HEADS * HEAD_DIM

    c_all = jnp.concatenate([c_prompt, c_sample, jnp.zeros((ADA_ROWS - bp - bs, d), F32)], axis=0)
    mods = ada_matmul(c_all, w_ada.reshape(depth * 2, d, 3 * d), b_ada.reshape(depth * 2, 1, 3 * d))
    mod_kv = _Group(ada_matmul(c_all, w_ada_kv[None], b_ada_kv[None, None])[0], bp, bs, ts)
    mod = [[_Group(mods[2 * layer + s], bp, bs, ts) for s in range(2)] for layer in range(depth)]

    def norm_pair(xp, xs, g, m, n, dtype):
        hp = norm_mod(xp, g[None], m.prompt(1, n), m.prompt(0, n), rows_per_batch=tp, per_row=False,
                      out_dtype=dtype, tm=ROW_TILE)
        hs = norm_mod(xs, g[None], m.sample(1, n), m.sample(0, n), rows_per_batch=ts, per_row=True,
                      out_dtype=dtype, tm=ms)
        return hp, hs

    xp = x_prompt.reshape(mp, d)
    xs = x_sample.reshape(ms, d)

    hp, hs = norm_pair(xp, xs, norm_g[0, 0], mod[0][0], 3, BF16)
    vp, conv_p = inproj_seq(hp, w_a_in[0], w_a_conv[0], jnp.zeros((bp, 2, d), F32), seq_len=tp, tm=ROW_TILE,
                            tn=COL_TILE)
    st = state_conv[0]
    s1 = jnp.repeat(st[:, 1], ts, axis=0)
    s2 = jnp.concatenate([st, jnp.zeros((bs, ts - 2, d), F32)], axis=1).reshape(ms, d)
    vs, us = inproj_rows(hs, w_a_in[0], w_a_conv[0], s1, s2, seq_len=ts, tn=COL_TILE)
    conv_s = us.reshape(bs, ts, d)[:, ts - 2:]
    xp = proj_residual(vp, w_a_out[0], xp, mod[0][0].prompt(2, 3), rows_per_batch=tp, per_row=False, tm=ROW_TILE,
                       tn=COL_TILE)
    xs = proj_residual(vs, w_a_out[0], xs, mod[0][0].sample(2, 3), rows_per_batch=ts, per_row=True, tm=ms,
                       tn=COL_TILE)
    xp, xs = _moe_block(xp, xs, norm_g[0, 1][None], mod[0][1], 0, tp, ts, w_router, b_router, w_gu, b_gu, w_down,
                        b_down)

    hp, hs = norm_pair(xp, xs, norm_kv, mod_kv, 2, BF16)
    tabs_p = rotary_tables(jnp.arange(tp))
    tabs_s = rotary_tables(past_len + jnp.arange(ms) % ts)
    kv_p, kvb_p = kv_proj(hp, w_kv, tabs_p, seq_len=tp, tm=ROW_TILE, head_major=True)
    (kv_s,) = kv_proj(hs, w_kv, tabs_s, seq_len=ts, tm=ms, head_major=False)
    own_pages = jnp.arange(mp // PAGE_SIZE, dtype=I32).reshape(bp, tp // PAGE_SIZE)
    cmp_p = compress(kv_p.reshape(mp // PAGE_SIZE, PAGE_SIZE, kv_p.shape[1]), own_pages, w_cmp1, w_cmp2, pe_cmp)
    cache_pages = cache_kv.reshape(cache_kv.shape[0], PAGE_SIZE, 2 * kvw)
    cmp_s = compress(cache_pages, page_table, w_cmp1, w_cmp2, pe_cmp)

    hp, hs = norm_pair(xp, xs, norm_g[1, 0], mod[1][0], 3, BF16)
    w_qg = w_b_qg[0]
    w_gate = jnp.pad(w_qg[:, hd:], ((0, 0), (0, LANES - N_HEADS * 3)))
    q_p = q_proj(hp, w_qg, tabs_p, seq_len=tp, tm=ROW_TILE, tn=COL_TILE, per_row=False)
    q_s = q_proj(hs, w_qg, tabs_s, seq_len=ts, tm=ms, tn=COL_TILE, per_row=True)
    gt_p = gate_proj(hp, w_gate, tm=ROW_TILE)
    gt_s = gate_proj(hs, w_gate, tm=ms)
    o_p = nsa_prompt(q_p, gt_p, kvb_p, cmp_p, seq_len=tp, tq=128)

    rows = ts * GROUP
    q_st = q_s.reshape(bs, ts, KV_HEADS, GROUP, HEAD_DIM).transpose(0, 2, 1, 3, 4).reshape(bs, KV_HEADS, rows, HEAD_DIM)
    gt_st = gt_s[:, :N_HEADS * 3].reshape(bs, ts, KV_HEADS, GROUP, 3).transpose(0, 2, 1, 3, 4)
    gt_st = jnp.pad(gt_st.reshape(bs, KV_HEADS, rows, 3), ((0, 0), (0, 0), (0, 0), (0, LANES - 3)))
    kv_s3 = kv_s.reshape(bs, ts, 3 * kvw)
    new_page = jnp.pad(kv_s3[:, :, kvw:2 * kvw], ((0, 0), (0, PAGE_SIZE - ts), (0, 0)))
    win_full = jnp.concatenate([cache_win.reshape(bs, n_buf, kvw), kv_s3[:, :, 2 * kvw:]], axis=1)
    win_pad = jnp.pad(win_full, ((0, 0), (0, WIN_PAD - n_buf - ts), (0, 0)))
    o_st = nsa_sample(q_st, gt_st, cmp_s, cache_pages, page_table, new_page, win_pad, n_tok=ts, n_buf=n_buf)
    o_s = o_st.reshape(bs, KV_HEADS, ts, GROUP, HEAD_DIM).transpose(0, 2, 1, 3, 4).reshape(ms, hd).astype(BF16)

    xp = proj_residual(o_p, w_b_o[0], xp, mod[1][0].prompt(2, 3), rows_per_batch=tp, per_row=False, tm=ROW_TILE,
                       tn=COL_TILE)
    xs = proj_residual(o_s, w_b_o[0], xs, mod[1][0].sample(2, 3), rows_per_batch=ts, per_row=True, tm=ms,
                       tn=COL_TILE)
    xp, xs = _moe_block(xp, xs, norm_g[1, 1][None], mod[1][1], 1, tp, ts, w_router, b_router, w_gu, b_gu, w_down,
                        b_down)

    y_prompt = plain_norm(xp, norm_f[None], ROW_TILE).reshape(bp, tp, d)
    y_sample = plain_norm(xs, norm_f[None], ms).reshape(bs, ts, d)
    kv_p3 = kv_p.reshape(bp, tp, 3 * kvw)
    kv_rows_prompt = kv_p3[:, :, :2 * kvw].reshape(bp, tp, 4, KV_HEADS, HEAD_DIM)
    kv_rows_sample = kv_s3[:, :, :2 * kvw].reshape(bs, ts, 4, KV_HEADS, HEAD_DIM)
    n_win = min(WINDOW, tp)
    win_prompt = kv_p3[:, tp - n_win:, 2 * kvw:].reshape(bp, n_win, 2, KV_HEADS, HEAD_DIM)
    win_sample = win_full[:, ts:].reshape(bs, n_buf, 2, KV_HEADS, HEAD_DIM)
    return (y_prompt, y_sample, kv_rows_prompt, kv_rows_sample, win_prompt, win_sample, conv_p[None], conv_s[None])
```
